```python
import math
import jax, jax.numpy as jnp
from jax import lax
import numpy as np

D_MODEL = 1024
BATCH = 32
SEQ = 2048
DEPTH = 4

H_A = 8
DA = 64
H_B = 8
DKB = 64
DVB = 128
H_C = 8
DC = 128
CONV_W = 3
N_EXPERTS = 16
D_FF = 2048
CAPACITY_FACTOR = 2
T5_BUCKETS = 32
T5_MAX_DIST = 128
Q_BLOCK = 128
CHUNK = 128
ROPE_BASE = 10000.0
EPS = 1e-6
NEG = -1e30

A_QK = H_A * 2 * DA
A_V = H_A * 2 * DA
B_QK = H_B * DKB
B_V = H_B * DVB
C_W = H_C * DC
C_GATES = 2 * 2 * H_C
IN_WIDTHS = (A_QK, A_QK, A_V, B_QK, B_QK, B_V, B_V, C_W, C_W, C_W, C_W, C_GATES, 3 * D_MODEL)
IN_COLS = sum(IN_WIDTHS)
IN_OFFSETS = tuple(int(v) for v in np.cumsum(IN_WIDTHS)[:-1])

kernel_name = "hybrid_diffattn_retention_mlstm_ecmoe_encoder"


def rms_norm(x, g):
    x32 = x.astype(jnp.float32)
    y = x32 * lax.rsqrt(jnp.mean(x32 * x32, axis=-1, keepdims=True) + EPS)
    return (y * g.astype(jnp.float32)).astype(x.dtype)


def heads(t, h):
    B, S, _ = t.shape
    return t.reshape(B, S, h, -1).transpose(0, 2, 1, 3)


def head_norm(h, gain, center, out_dtype):
    h32 = h.astype(jnp.float32)
    if center:
        h32 = h32 - jnp.mean(h32, axis=-1, keepdims=True)
    h32 = h32 * lax.rsqrt(jnp.mean(h32 * h32, axis=-1, keepdims=True) + EPS)
    B, H, S, dv = h.shape
    y = h32.transpose(0, 2, 1, 3).reshape(B, S, H * dv) * gain.astype(jnp.float32)
    return y.astype(out_dtype)


def rotary(t, pos):
    half = t.shape[-1] // 2
    freqs = ROPE_BASE ** (-jnp.arange(half, dtype=jnp.float32) / half)
    ang = pos[:, None] * freqs[None, :]
    cos = jnp.cos(ang).astype(t.dtype)
    sin = jnp.sin(ang).astype(t.dtype)
    t1, t2 = t[..., :half], t[..., half:]
    return jnp.concatenate([t1 * cos - t2 * sin, t1 * sin + t2 * cos], axis=-1)


def t5_bucket(rel):
    half = T5_BUCKETS // 2
    max_exact = half // 2
    offset = jnp.where(rel > 0, half, 0)
    n = jnp.abs(rel)
    large = max_exact + (jnp.log(jnp.maximum(n, 1).astype(jnp.float32) / max_exact)
                         / math.log(T5_MAX_DIST / max_exact) * (half - max_exact)).astype(jnp.int32)
    large = jnp.minimum(large, half - 1)
    return offset + jnp.where(n < max_exact, n, large)


def diff_attention(aq, ak, av, t5_table, lam):
    B, H, S, _ = aq.shape
    nb = S // Q_BLOCK
    q = jnp.stack([aq[..., :DA], aq[..., DA:]], axis=2) * (DA ** -0.5)
    k = jnp.stack([ak[..., :DA], ak[..., DA:]], axis=2)
    qb = q.reshape(B, H, 2, nb, Q_BLOCK, DA).transpose(3, 0, 1, 2, 4, 5)
    starts = jnp.arange(nb, dtype=jnp.int32) * Q_BLOCK
    kpos = jnp.arange(S, dtype=jnp.int32)

    def one_block(args):
        q_blk, start = args
        s = jnp.einsum('bhcqd,bhckd->bhcqk', q_blk, k).astype(jnp.float32)
        qpos = start + jnp.arange(Q_BLOCK, dtype=jnp.int32)
        bias = t5_table[t5_bucket(kpos[None, :] - qpos[:, None])]
        s = s + jnp.transpose(bias, (2, 0, 1)).astype(jnp.float32)[None, :, None]
        p = jax.nn.softmax(s, axis=-1)
        a = p[:, :, 0] - lam * p[:, :, 1]
        return jnp.einsum('bhqk,bhkv->bhqv', a.astype(av.dtype), av)

    o = lax.map(one_block, (qb, starts))
    return o.transpose(1, 2, 0, 3, 4).reshape(B, H, S, 2 * DA)


def retention_dir(q, k, v, log_gamma):
    B, H, S, dk = q.shape
    dv = v.shape[-1]
    n = S // CHUNK
    to_chunks = lambda t: t.reshape(B, H, n, CHUNK, t.shape[-1]).transpose(2, 0, 1, 3, 4)
    idx = jnp.arange(CHUNK, dtype=jnp.float32)
    diff = idx[:, None] - idx[None, :]
    lg = log_gamma[:, None, None]
    dmat = jnp.where(diff[None] >= 0, jnp.exp(lg * jnp.maximum(diff, 0.0)[None]), 0.0)
    xi = jnp.exp(log_gamma[:, None] * (idx[None, :] + 1.0))
    zeta = jnp.exp(log_gamma[:, None] * (CHUNK - 1.0 - idx[None, :]))
    g_chunk = jnp.exp(log_gamma * CHUNK)

    def step(R, inp):
        qi, ki, vi = inp
        inner = jnp.einsum('bhld,bhmd->bhlm', qi, ki) * dmat[None]
        o = (jnp.einsum('bhlm,bhmv->bhlv', inner, vi)
             + jnp.einsum('bhld,bhdv->bhlv', qi * xi[None, :, :, None], R))
        R = g_chunk[None, :, None, None] * R + jnp.einsum('bhmd,bhmv->bhdv', ki * zeta[None, :, :, None], vi)
        return R, o

    R0 = jnp.zeros((B, H, dk, dv), jnp.float32)
    _, o = lax.scan(step, R0, (to_chunks(q), to_chunks(k), to_chunks(v)))
    return o.transpose(1, 2, 0, 3, 4).reshape(B, H, S, dv)


def mlstm_dir(q, k, v, i_pre, f_pre):
    B, H, S, d = q.shape
    n = S // CHUNK
    to_chunks = lambda t: t.reshape(B, H, n, CHUNK, -1).transpose(2, 0, 1, 3, 4)
    gate_chunks = lambda t: t.reshape(B, H, n, CHUNK).transpose(2, 0, 1, 3)
    log_f = jax.nn.log_sigmoid(f_pre)
    causal = jnp.tril(jnp.ones((CHUNK, CHUNK), dtype=bool))

    def step(carry, inp):
        C, nv, m = carry
        qi, ki, vi, ii, lfi = inp
        b = jnp.cumsum(lfi, axis=-1)
        dlog = jnp.where(causal, b[..., :, None] - b[..., None, :] + ii[..., None, :], NEG)
        inter = b + m[..., None]
        m_row = jnp.maximum(jnp.max(dlog, axis=-1), inter)
        w_in = jnp.exp(dlog - m_row[..., None])
        w_state = jnp.exp(inter - m_row)
        s = jnp.einsum('bhld,bhmd->bhlm', qi, ki) * w_in
        num = jnp.einsum('bhlm,bhmv->bhlv', s, vi) + w_state[..., None] * jnp.einsum('bhld,bhdv->bhlv', qi, C)
        den = jnp.sum(s, axis=-1) + w_state * jnp.einsum('bhld,bhd->bhl', qi, nv)
        h = num / jnp.maximum(jnp.abs(den), jnp.exp(-m_row))[..., None]
        b_last = b[..., -1]
        wlog = b_last[..., None] - b + ii
        m_new = jnp.maximum(b_last + m, jnp.max(wlog, axis=-1))
        w = jnp.exp(wlog - m_new[..., None])
        decay = jnp.exp(b_last + m - m_new)
        C = decay[..., None, None] * C + jnp.einsum('bhl,bhld,bhlv->bhdv', w, ki, vi)
        nv = decay[..., None] * nv + jnp.einsum('bhl,bhld->bhd', w, ki)
        return (C, nv, m_new), h

    init = (jnp.zeros((B, H, d, d), jnp.float32), jnp.zeros((B, H, d), jnp.float32),
            jnp.full((B, H), NEG, jnp.float32))
    _, h = lax.scan(step, init, (to_chunks(q), to_chunks(k), to_chunks(v), gate_chunks(i_pre), gate_chunks(log_f)))
    return h.transpose(1, 2, 0, 3, 4).reshape(B, H, S, d)


def centred_depthwise_conv(u, w):
    return lax.conv_general_dilated(u, w.astype(u.dtype)[:, None, :], window_strides=(1,), padding='SAME',
                                    dimension_numbers=('NWC', 'WIO', 'NWC'), feature_group_count=u.shape[-1])


def expert_choice_ffn(x, w_router, w_gate, w_up, w_down):
    B, S, D = x.shape
    cap = CAPACITY_FACTOR * S // N_EXPERTS
    aff = jax.nn.softmax(jnp.einsum('bsd,de->bse', x, w_router).astype(jnp.float32), axis=-1)
    g, idx = lax.top_k(jnp.transpose(aff, (0, 2, 1)), cap)
    xe = jax.vmap(lambda xb, ib: xb[ib])(x, idx)
    h = jax.nn.silu(jnp.einsum('becd,edf->becf', xe, w_gate)) * jnp.einsum('becd,edf->becf', xe, w_up)
    y = (jnp.einsum('becf,efd->becd', h, w_down) * g[..., None].astype(x.dtype)).astype(x.dtype)
    return jax.vmap(lambda ib, yb: jnp.zeros((S, D), yb.dtype).at[ib.reshape(-1)].add(yb.reshape(-1, D)))(idx, y)


def setup_inputs(seed: int = 0) -> dict:
    key = jax.random.key(seed)
    ks = jax.random.split(key, 24)
    f32 = jnp.float32
    nrm = lambda k, shape, scale: jax.random.normal(k, shape, f32) * scale
    gain = lambda k, shape: 1.0 + nrm(k, shape, 0.02)
    gamma0 = 1.0 - 2.0 ** (-5.0 - jnp.arange(H_B, dtype=f32))
    logit0 = jnp.log(gamma0) - jnp.log1p(-gamma0)
    forget_init = jnp.linspace(3.0, 6.0, H_C, dtype=f32)
    gate_sel = jnp.array([0.0, 1.0], f32)
    return {
        "x": nrm(ks[0], (BATCH, SEQ, D_MODEL), 1.0),
        "t5_bias": nrm(ks[1], (T5_BUCKETS, H_A), 0.2),
        "attn_norm_g": gain(ks[2], (DEPTH, D_MODEL)),
        "w_in": nrm(ks[3], (DEPTH, D_MODEL, IN_COLS), D_MODEL ** -0.5),
        "branch_gate_b": nrm(ks[4], (DEPTH, 3, D_MODEL), 0.02),
        "diff_lambda": nrm(ks[5], (DEPTH, 4, DA), 0.1),
        "diff_norm_g": gain(ks[6], (DEPTH, A_V)),
        "ret_decay_logit": logit0[None, None, :] + nrm(ks[7], (DEPTH, 2, H_B), 0.01),
        "ret_norm_g": gain(ks[8], (DEPTH, B_V)),
        "mlstm_conv_w": nrm(ks[9], (DEPTH, CONV_W, 2 * C_W), CONV_W ** -0.5),
        "mlstm_gate_b": nrm(ks[10], (DEPTH, 2, 2, H_C), 0.1) + gate_sel[None, None, :, None] * forget_init[None, None, None, :],
        "mlstm_norm_g": gain(ks[11], (DEPTH, C_W)),
        "w_branch_a": nrm(ks[12], (DEPTH, A_V, D_MODEL), A_V ** -0.5),
        "w_branch_b": nrm(ks[13], (DEPTH, B_V, D_MODEL), B_V ** -0.5),
        "w_branch_c": nrm(ks[14], (DEPTH, C_W, D_MODEL), C_W ** -0.5),
        "w_out": nrm(ks[15], (DEPTH, D_MODEL, D_MODEL), D_MODEL ** -0.5),
        "ffn_norm_g": gain(ks[16], (DEPTH, D_MODEL)),
        "w_router": nrm(ks[17], (DEPTH, D_MODEL, N_EXPERTS), D_MODEL ** -0.5),
        "w_exp_gate": nrm(ks[18], (DEPTH, N_EXPERTS, D_MODEL, D_FF), D_MODEL ** -0.5),
        "w_exp_up": nrm(ks[19], (DEPTH, N_EXPERTS, D_MODEL, D_FF), D_MODEL ** -0.5),
        "w_exp_down": nrm(ks[20], (DEPTH, N_EXPERTS, D_FF, D_MODEL), D_FF ** -0.5),
        "final_norm_g": gain(ks[21], (D_MODEL,)),
    }


def reference(x, t5_bias, attn_norm_g, w_in, branch_gate_b, diff_lambda, diff_norm_g, ret_decay_logit,
              ret_norm_g, mlstm_conv_w, mlstm_gate_b, mlstm_norm_g, w_branch_a, w_branch_b, w_branch_c,
              w_out, ffn_norm_g, w_router, w_exp_gate, w_exp_up, w_exp_down, final_norm_g):
    B, S, _ = x.shape
    f32 = jnp.float32
    pos = jnp.arange(S, dtype=f32)
    flip = lambda t: jnp.flip(t, axis=2)
    for l in range(DEPTH):
        xn = rms_norm(x, attn_norm_g[l])
        z = xn @ w_in[l]
        (aq, ak, av, bq, bk, bv, bg, cq, ck, cv, co, cgt, bgt) = jnp.split(z, IN_OFFSETS, axis=-1)

        lam_init = 0.8 - 0.6 * math.exp(-0.3 * l)
        lq1, lk1, lq2, lk2 = diff_lambda[l].astype(f32)
        lam = jnp.exp(jnp.sum(lq1 * lk1)) - jnp.exp(jnp.sum(lq2 * lk2)) + lam_init
        oa = diff_attention(heads(aq, H_A), heads(ak, H_A), heads(av, H_A), t5_bias, lam)
        y_a = head_norm(oa, diff_norm_g[l], False, x.dtype) * (1.0 - lam_init)

        rq = rotary(heads(bq, H_B), pos).astype(f32)
        rk = (rotary(heads(bk, H_B), pos) * (DKB ** -0.5)).astype(f32)
        rv = heads(bv, H_B).astype(f32)
        lg = jax.nn.log_sigmoid(ret_decay_logit[l].astype(f32))
        ob = retention_dir(rq, rk, rv, lg[0]) + flip(retention_dir(flip(rq), flip(rk), flip(rv), lg[1]))
        y_b = jax.nn.silu(bg) * head_norm(ob, ret_norm_g[l], True, x.dtype)

        qk = jax.nn.silu(centred_depthwise_conv(jnp.concatenate([cq, ck], axis=-1), mlstm_conv_w[l]))
        mq = heads(qk[..., :C_W], H_C).astype(f32)
        mk = (heads(qk[..., C_W:], H_C) * (DC ** -0.5)).astype(f32)
        mv = heads(cv, H_C).astype(f32)
        gts = (cgt.reshape(B, S, 2, 2, H_C) + mlstm_gate_b[l]).astype(f32).transpose(2, 3, 0, 4, 1)
        hc = (mlstm_dir(mq, mk, mv, gts[0, 0], gts[0, 1])
              + flip(mlstm_dir(flip(mq), flip(mk), flip(mv), flip(gts[1, 0]), flip(gts[1, 1]))))
        y_c = jax.nn.sigmoid(co) * head_norm(hc, mlstm_norm_g[l], True, x.dtype)

        gt = jax.nn.sigmoid((bgt.reshape(B, S, 3, D_MODEL) + branch_gate_b[l]).astype(f32)).astype(x.dtype)
        mixed = (gt[:, :, 0] * (y_a @ w_branch_a[l]) + gt[:, :, 1] * (y_b @ w_branch_b[l])
                 + gt[:, :, 2] * (y_c @ w_branch_c[l]))
        x = x + mixed @ w_out[l]

        x = x + expert_choice_ffn(rms_norm(x, ffn_norm_g[l]), w_router[l], w_exp_gate[l], w_exp_up[l], w_exp_down[l])
    return rms_norm(x, final_norm_g)
```

```python
import functools
import math

import jax
import jax.numpy as jnp
from jax import lax
from jax.experimental import pallas as pl
from jax.experimental.pallas import tpu as pltpu

F32 = jnp.float32
BF16 = jnp.bfloat16

D_MODEL = 1024
N_HEADS = 8
HEAD_W = 128
DA = 64
DKB = 64
CHUNK = 128
CONV_W = 3
N_EXPERTS = 16
D_FF = 2048
CAPACITY_FACTOR = 2
T5_BUCKETS = 32
T5_MAX_DIST = 128
ROPE_BASE = 10000.0
EPS = 1e-6
NEG = -1e30
GATE_COLS = 32
LANES = 128

VMEM_LIMIT = 56 * 1024 * 1024

NT_DIMS = (((1,), (1,)), ((), ()))
TN_DIMS = (((0,), (0,)), ((), ()))


def _params(sem):
    return pltpu.CompilerParams(dimension_semantics=sem, vmem_limit_bytes=VMEM_LIMIT)


def _log_sigmoid(x):
    return -(jnp.maximum(-x, 0.0) + jnp.log1p(jnp.exp(-jnp.abs(x))))


def _sigmoid(x):
    return 1.0 / (1.0 + jnp.exp(-x))


def _silu(x):
    return x * _sigmoid(x)


def _rms(x, g):
    return x * lax.rsqrt(jnp.mean(x * x, axis=-1, keepdims=True) + EPS) * g


def _layer_norm_head(h, gain):
    h = h - jnp.mean(h, axis=-1, keepdims=True)
    return h * lax.rsqrt(jnp.mean(h * h, axis=-1, keepdims=True) + EPS) * gain


def _norm_matmul_kernel(x_ref, g_ref, w_ref, o_ref, xn_ref):
    @pl.when(pl.program_id(1) == 0)
    def _():
        xn_ref[...] = _rms(x_ref[...], g_ref[...]).astype(BF16)

    o_ref[...] = jnp.dot(xn_ref[...], w_ref[...], preferred_element_type=F32).astype(o_ref.dtype)


def _norm_matmul(x, g, w, out_dtype, tm, tn):
    T, D = x.shape
    N = w.shape[1]
    return pl.pallas_call(
        _norm_matmul_kernel,
        grid=(T // tm, N // tn),
        in_specs=[
            pl.BlockSpec((tm, D), lambda i, j: (i, 0)),
            pl.BlockSpec((1, D), lambda i, j: (0, 0)),
            pl.BlockSpec((D, tn), lambda i, j: (0, j)),
        ],
        out_specs=pl.BlockSpec((tm, tn), lambda i, j: (i, j)),
        out_shape=jax.ShapeDtypeStruct((T, N), out_dtype),
        scratch_shapes=[pltpu.VMEM((tm, D), BF16)],
        compiler_params=_params(("parallel", "arbitrary")),
        name="norm_matmul",
    )(x, g, w)


def _t5_bias_kernel(tab_ref, bkt_ref, o_ref, *, tq, S):
    h = pl.program_id(0)
    qi = pl.program_id(1)
    bkt = bkt_ref[...]
    r = jnp.zeros((1, 2 * S), F32)
    for b in range(T5_BUCKETS):
        r = jnp.where(bkt == b, tab_ref[b, h], r)
    x = jnp.broadcast_to(r, (tq, 2 * S))
    rolled = pltpu.roll(x, qi * tq + S, 1, stride=1, stride_axis=0)
    o_ref[0] = rolled[:, :S]


def _t5_bucket(rel):
    half = T5_BUCKETS // 2
    max_exact = half // 2
    offset = jnp.where(rel > 0, half, 0)
    n = jnp.abs(rel)
    large = max_exact + (jnp.log(jnp.maximum(n, 1).astype(F32) / max_exact)
                         / math.log(T5_MAX_DIST / max_exact) * (half - max_exact)).astype(jnp.int32)
    large = jnp.minimum(large, half - 1)
    return offset + jnp.where(n < max_exact, n, large)


def _t5_bias(table, S, tq):
    rel = jnp.arange(2 * S, dtype=jnp.int32) - S
    bkt = _t5_bucket(rel).astype(jnp.int32).reshape(1, 2 * S)
    return pl.pallas_call(
        functools.partial(_t5_bias_kernel, tq=tq, S=S),
        grid=(N_HEADS, S // tq),
        in_specs=[
            pl.BlockSpec(memory_space=pltpu.SMEM),
            pl.BlockSpec((1, 2 * S), lambda h, q: (0, 0)),
        ],
        out_specs=pl.BlockSpec((1, tq, S), lambda h, q: (h, q, 0)),
        out_shape=jax.ShapeDtypeStruct((N_HEADS, S, S), F32),
        compiler_params=_params(("parallel", "arbitrary")),
        name="t5_bias",
    )(table, bkt)


def _attn_kernel(dl_ref, q_ref, k_ref, v_ref, b_ref, g_ref, o_ref, *, lam_init):
    dl = dl_ref[...]
    lam = (jnp.exp(jnp.sum(dl[0:1] * dl[1:2], axis=-1, keepdims=True))
           - jnp.exp(jnp.sum(dl[2:3] * dl[3:4], axis=-1, keepdims=True)) + lam_init)
    q = q_ref[0] * (DA ** -0.5)
    lane = lax.broadcasted_iota(jnp.int32, q.shape, 1)
    k = k_ref[0]
    v = v_ref[0]
    bias = b_ref[0]

    def softmax_half(qh):
        s = lax.dot_general(qh, k, NT_DIMS, preferred_element_type=F32) + bias
        e = jnp.exp(s - jnp.max(s, axis=-1, keepdims=True))
        l = jnp.sum(e, axis=-1, keepdims=True)
        return jnp.dot(e.astype(BF16), v, preferred_element_type=F32) / l

    o = (softmax_half(jnp.where(lane < DA, q, jnp.zeros_like(q)))
         - lam * softmax_half(jnp.where(lane >= DA, q, jnp.zeros_like(q))))
    y = o * lax.rsqrt(jnp.mean(o * o, axis=-1, keepdims=True) + EPS) * g_ref[...] * (1.0 - lam_init)
    o_ref[0] = y.astype(o_ref.dtype)


def _diff_attention(za, bias, dl, gain, lam_init, B, S, tq):
    return pl.pallas_call(
        functools.partial(_attn_kernel, lam_init=lam_init),
        grid=(N_HEADS, S // tq, B),
        in_specs=[
            pl.BlockSpec((4, DA), lambda h, q, b: (0, 0)),
            pl.BlockSpec((1, tq, HEAD_W), lambda h, q, b: (b, q, h)),
            pl.BlockSpec((1, S, HEAD_W), lambda h, q, b: (b, 0, N_HEADS + h)),
            pl.BlockSpec((1, S, HEAD_W), lambda h, q, b: (b, 0, 2 * N_HEADS + h)),
            pl.BlockSpec((1, tq, S), lambda h, q, b: (h, q, 0)),
            pl.BlockSpec((1, HEAD_W), lambda h, q, b: (0, h)),
        ],
        out_specs=pl.BlockSpec((1, tq, HEAD_W), lambda h, q, b: (b, q, h)),
        out_shape=jax.ShapeDtypeStruct((B, S, D_MODEL), BF16),
        compiler_params=_params(("parallel", "parallel", "arbitrary")),
        name="diff_attention",
    )(dl, za, za, za, bias, gain)


def _retention_kernel(dec_ref, q_ref, k_ref, v_ref, bg_ref, cos_ref, sin_ref, g_ref, o_ref,
                      qs_ref, ks_ref, acc_ref, r_ref, *, S):
    L = CHUNK
    n = S // L
    h = pl.program_id(1)
    lane = lax.broadcasted_iota(jnp.int32, (S, LANES), 1)
    head_mask = (lane // DKB) == (h % 2)
    first_half = (lane & (DKB // 2)) == 0
    cos = cos_ref[...]
    sin = sin_ref[...]

    def rotary(t):
        partner = jnp.where(first_half, pltpu.roll(t, LANES - DKB // 2, 1), pltpu.roll(t, DKB // 2, 1))
        return t * cos + partner * sin

    qs_ref[...] = jnp.where(head_mask, rotary(q_ref[0]), 0.0)
    ks_ref[...] = jnp.where(head_mask, rotary(k_ref[0]) * (DKB ** -0.5), 0.0)

    lg = _log_sigmoid(dec_ref[0])
    lgf, lgb = lg[0:1], lg[1:2]
    row = lax.broadcasted_iota(jnp.int32, (L, LANES), 0).astype(F32)
    col = lax.broadcasted_iota(jnp.int32, (L, LANES), 1).astype(F32)
    diff = row - col
    dmat = (jnp.where(diff >= 0, jnp.exp(lgf * jnp.maximum(diff, 0.0)), 0.0)
            + jnp.where(diff <= 0, jnp.exp(lgb * jnp.maximum(-diff, 0.0)), 0.0))
    xi_f = jnp.exp(lgf * (row + 1.0))
    zeta_f = jnp.exp(lgf * (L - 1.0 - row))
    xi_b = jnp.exp(lgb * (L - row))
    zeta_b = jnp.exp(lgb * row)
    decay_f = jnp.exp(lgf * L)
    decay_b = jnp.exp(lgb * L)

    r_ref[...] = jnp.zeros_like(r_ref)

    def forward(c, carry):
        sl = pl.ds(pl.multiple_of(c * L, L), L)
        qc, kc, vc = qs_ref[sl, :], ks_ref[sl, :], v_ref[0, sl, :]
        inner = lax.dot_general(qc.astype(BF16), kc.astype(BF16), NT_DIMS, preferred_element_type=F32) * dmat
        acc_ref[sl, :] = (jnp.dot(inner.astype(BF16), vc, preferred_element_type=F32)
                          + jnp.dot((qc * xi_f).astype(BF16), r_ref[...].astype(BF16), preferred_element_type=F32))
        r_ref[...] = decay_f * r_ref[...] + lax.dot_general(
            (kc * zeta_f).astype(BF16), vc, TN_DIMS, preferred_element_type=F32)
        return carry

    lax.fori_loop(0, n, forward, 0)
    r_ref[...] = jnp.zeros_like(r_ref)

    def backward(i, carry):
        c = n - 1 - i
        sl = pl.ds(pl.multiple_of(c * L, L), L)
        qc, kc, vc = qs_ref[sl, :], ks_ref[sl, :], v_ref[0, sl, :]
        acc_ref[sl, :] += jnp.dot((qc * xi_b).astype(BF16), r_ref[...].astype(BF16), preferred_element_type=F32)
        r_ref[...] = decay_b * r_ref[...] + lax.dot_general(
            (kc * zeta_b).astype(BF16), vc, TN_DIMS, preferred_element_type=F32)
        return carry

    lax.fori_loop(0, n, backward, 0)
    y = _silu(bg_ref[0]) * _layer_norm_head(acc_ref[...], g_ref[...])
    o_ref[0] = y.astype(o_ref.dtype)


def _retention(za, zb, dec, cos, sin, gain, B, S):
    qk_blk = 3 * N_HEADS
    return pl.pallas_call(
        functools.partial(_retention_kernel, S=S),
        grid=(B, N_HEADS),
        in_specs=[
            pl.BlockSpec((1, 2, LANES), lambda b, h: (h, 0, 0)),
            pl.BlockSpec((1, S, LANES), lambda b, h: (b, 0, qk_blk + h // 2)),
            pl.BlockSpec((1, S, LANES), lambda b, h: (b, 0, qk_blk + N_HEADS // 2 + h // 2)),
            pl.BlockSpec((1, S, HEAD_W), lambda b, h: (b, 0, 3 * N_HEADS + h)),
            pl.BlockSpec((1, S, HEAD_W), lambda b, h: (b, 0, 4 * N_HEADS + h)),
            pl.BlockSpec((S, LANES), lambda b, h: (0, 0)),
            pl.BlockSpec((S, LANES), lambda b, h: (0, 0)),
            pl.BlockSpec((1, HEAD_W), lambda b, h: (0, h)),
        ],
        out_specs=pl.BlockSpec((1, S, HEAD_W), lambda b, h: (b, 0, h)),
        out_shape=jax.ShapeDtypeStruct((B, S, D_MODEL), BF16),
        scratch_shapes=[pltpu.VMEM((S, LANES), F32), pltpu.VMEM((S, LANES), F32),
                        pltpu.VMEM((S, HEAD_W), F32), pltpu.VMEM((LANES, HEAD_W), F32)],
        compiler_params=_params(("parallel", "arbitrary")),
        name="retention",
    )(dec, zb, zb, za, zb, cos, sin, gain)


def _row_to_col(row, eye):
    return jnp.sum(jnp.where(eye, row, 0.0), axis=1, keepdims=True)


def _mlstm_kernel(q_ref, k_ref, v_ref, gt_ref, gb_ref, cwq_ref, cwk_ref, co_ref, gn_ref, o_ref,
                  qs_ref, ks_ref, acc_ref, gsc_ref, c_ref, n_ref, m_ref, *, S):
    L = CHUNK
    n = S // L
    h = pl.program_id(1)

    rows = lax.broadcasted_iota(jnp.int32, (S, HEAD_W), 0)

    def conv_silu(u, w):
        prev = jnp.where(rows == 0, 0.0, pltpu.roll(u, 1, 0))
        nxt = jnp.where(rows == S - 1, 0.0, pltpu.roll(u, S - 1, 0))
        return _silu(w[0:1] * prev + w[1:2] * u + w[2:3] * nxt)

    qs_ref[...] = conv_silu(q_ref[0], cwq_ref[...])
    ks_ref[...] = conv_silu(k_ref[0], cwk_ref[...]) * (HEAD_W ** -0.5)

    gsub = lax.broadcasted_iota(jnp.int32, (GATE_COLS, L), 0)

    def gate_rows(c, carry):
        sl = pl.ds(pl.multiple_of(c * L, L), L)
        gt = (gt_ref[0, sl, :] + gb_ref[...]).T[:GATE_COLS]
        pick = lambda j: jnp.sum(jnp.where(gsub == j, gt, 0.0), axis=0, keepdims=True)
        gsc_ref[c] = jnp.concatenate(
            [pick(h), _log_sigmoid(pick(N_HEADS + h)),
             pick(2 * N_HEADS + h), _log_sigmoid(pick(3 * N_HEADS + h)),
             jnp.zeros((4, L), F32)], axis=0)
        return carry

    lax.fori_loop(0, n, gate_rows, 0)

    ri = lax.broadcasted_iota(jnp.int32, (L, L), 0)
    ci = lax.broadcasted_iota(jnp.int32, (L, L), 1)
    eye = ri == ci

    def direction(rev):
        causal = (ci >= ri) if rev else (ci <= ri)
        cum = jnp.where(causal, 1.0, 0.0).T
        c_ref[...] = jnp.zeros_like(c_ref)
        n_ref[...] = jnp.zeros_like(n_ref)
        m_ref[...] = jnp.full_like(m_ref, NEG)

        def step(i, carry):
            c = (n - 1 - i) if rev else i
            sl = pl.ds(pl.multiple_of(c * L, L), L)
            qc, kc, vc = qs_ref[sl, :], ks_ref[sl, :], v_ref[0, sl, :]
            g = gsc_ref[c]
            ii = g[2:3] if rev else g[0:1]
            lf = g[3:4] if rev else g[1:2]
            m = m_ref[...]
            b_row = jnp.dot(jnp.broadcast_to(lf, (8, L)), cum, preferred_element_type=F32,
                            precision=lax.Precision.HIGHEST)[0:1]
            b_col = _row_to_col(b_row, eye)
            dlog = jnp.where(causal, b_col - b_row + ii, NEG)
            inter = b_col + m
            m_row = jnp.maximum(jnp.max(dlog, axis=1, keepdims=True), inter)
            w_in = jnp.exp(dlog - m_row)
            w_state = jnp.exp(inter - m_row)
            qb = qc.astype(BF16)
            s = lax.dot_general(qb, kc.astype(BF16), NT_DIMS, preferred_element_type=F32) * w_in
            num = (jnp.dot(s.astype(BF16), vc, preferred_element_type=F32)
                   + w_state * jnp.dot(qb, c_ref[...].astype(BF16), preferred_element_type=F32))
            den = (jnp.sum(s, axis=1, keepdims=True)
                   + w_state * jnp.sum(qc * n_ref[...], axis=1, keepdims=True))
            hh = num / jnp.maximum(jnp.abs(den), jnp.exp(-m_row))
            if rev:
                acc_ref[sl, :] += hh
            else:
                acc_ref[sl, :] = hh
            b_last = jnp.sum(lf, axis=1, keepdims=True)
            wlog = b_last - b_row + ii
            m_new = jnp.maximum(b_last + m, jnp.max(wlog, axis=1, keepdims=True))
            w_col = _row_to_col(jnp.exp(wlog - m_new), eye)
            decay = jnp.exp(b_last + m - m_new)
            kw = kc * w_col
            c_ref[...] = decay * c_ref[...] + lax.dot_general(
                kw.astype(BF16), vc, TN_DIMS, preferred_element_type=F32)
            n_ref[...] = decay * n_ref[...] + jnp.sum(kw, axis=0, keepdims=True)
            m_ref[...] = m_new
            return carry

        lax.fori_loop(0, n, step, 0)

    direction(False)
    direction(True)
    y = _sigmoid(co_ref[0]) * _layer_norm_head(acc_ref[...], gn_ref[...])
    o_ref[0] = y.astype(o_ref.dtype)


def _mlstm(za, zb, zg, gate_b, cwq, cwk, gain, B, S):
    return pl.pallas_call(
        functools.partial(_mlstm_kernel, S=S),
        grid=(B, N_HEADS),
        in_specs=[
            pl.BlockSpec((1, S, HEAD_W), lambda b, h: (b, 0, 5 * N_HEADS + h)),
            pl.BlockSpec((1, S, HEAD_W), lambda b, h: (b, 0, 6 * N_HEADS + h)),
            pl.BlockSpec((1, S, HEAD_W), lambda b, h: (b, 0, 4 * N_HEADS + h)),
            pl.BlockSpec((1, S, LANES), lambda b, h: (b, 0, 0)),
            pl.BlockSpec((1, LANES), lambda b, h: (0, 0)),
            pl.BlockSpec((CONV_W, HEAD_W), lambda b, h: (0, h)),
            pl.BlockSpec((CONV_W, HEAD_W), lambda b, h: (0, h)),
            pl.BlockSpec((1, S, HEAD_W), lambda b, h: (b, 0, 7 * N_HEADS + h)),
            pl.BlockSpec((1, HEAD_W), lambda b, h: (0, h)),
        ],
        out_specs=pl.BlockSpec((1, S, HEAD_W), lambda b, h: (b, 0, h)),
        out_shape=jax.ShapeDtypeStruct((B, S, D_MODEL), BF16),
        scratch_shapes=[pltpu.VMEM((S, HEAD_W), F32), pltpu.VMEM((S, HEAD_W), F32),
                        pltpu.VMEM((S, HEAD_W), F32), pltpu.VMEM((S // CHUNK, 8, CHUNK), F32),
                        pltpu.VMEM((HEAD_W, HEAD_W), F32), pltpu.VMEM((1, HEAD_W), F32),
                        pltpu.VMEM((1, 1), F32)],
        compiler_params=_params(("parallel", "arbitrary")),
        name="mlstm",
    )(zb, zb, za, zg, gate_b, cwq, cwk, zb, gain)


def _merge_kernel(x_ref, ya_ref, yb_ref, yc_ref, bgt_ref, gb_ref, wa_ref, wb_ref, wc_ref, wo_ref,
                  g2_ref, wr_ref, xo_ref, xn_ref, lg_ref):
    D = D_MODEL
    gate = lambda j: _sigmoid(bgt_ref[:, j * D:(j + 1) * D] + gb_ref[j:j + 1, :])
    mixed = (gate(0) * jnp.dot(ya_ref[...], wa_ref[...], preferred_element_type=F32)
             + gate(1) * jnp.dot(yb_ref[...], wb_ref[...], preferred_element_type=F32)
             + gate(2) * jnp.dot(yc_ref[...], wc_ref[...], preferred_element_type=F32))
    x = x_ref[...] + jnp.dot(mixed.astype(BF16), wo_ref[...], preferred_element_type=F32)
    xo_ref[...] = x
    xn = _rms(x, g2_ref[...])
    xn_ref[...] = xn.astype(BF16)
    lg_ref[...] = jnp.dot(xn, wr_ref[...], preferred_element_type=F32, precision=lax.Precision.HIGHEST)


def _merge(x, ya, yb, yc, zb, gate_b, wa, wb, wc, wo, g2, wr, tm):
    T, D = x.shape
    row = lambda i: (i, 0)
    const = lambda i: (0, 0)
    return pl.pallas_call(
        _merge_kernel,
        grid=(T // tm,),
        in_specs=[
            pl.BlockSpec((tm, D), row), pl.BlockSpec((tm, D), row), pl.BlockSpec((tm, D), row),
            pl.BlockSpec((tm, D), row), pl.BlockSpec((tm, 3 * D), row), pl.BlockSpec((3, D), const),
            pl.BlockSpec((D, D), const), pl.BlockSpec((D, D), const), pl.BlockSpec((D, D), const),
            pl.BlockSpec((D, D), const), pl.BlockSpec((1, D), const), pl.BlockSpec((D, LANES), const),
        ],
        out_specs=[pl.BlockSpec((tm, D), row), pl.BlockSpec((tm, D), row), pl.BlockSpec((tm, LANES), row)],
        out_shape=[jax.ShapeDtypeStruct((T, D), F32), jax.ShapeDtypeStruct((T, D), BF16),
                   jax.ShapeDtypeStruct((T, LANES), F32)],
        compiler_params=_params(("parallel",)),
        name="merge",
    )(x, ya, yb, yc, zb, gate_b, wa, wb, wc, wo, g2, wr)


def _router_kernel(lg_ref, tri_ref, aff_ref, pos_ref, *, cap):
    E = N_EXPERTS
    logits = lg_ref[0].T[:E]
    e = jnp.exp(logits - jnp.max(logits, axis=0, keepdims=True))
    aff = e / jnp.sum(e, axis=0, keepdims=True)
    key = pltpu.bitcast(aff, jnp.int32)

    def refine(i, thr):
        cand = thr | (jnp.int32(1) << (30 - i))
        cnt = jnp.sum(jnp.where(key >= cand, 1.0, 0.0), axis=1, keepdims=True)
        return jnp.where(cnt >= cap, cand, thr)

    thr = lax.fori_loop(0, 31, refine, jnp.zeros((E, 1), jnp.int32))
    above = key > thr
    tied = key == thr
    need = cap - jnp.sum(jnp.where(above, 1.0, 0.0), axis=1, keepdims=True)
    tri = tri_ref[...]
    tie_rank = jnp.dot(jnp.where(tied, 1.0, 0.0).astype(BF16), tri, preferred_element_type=F32)
    chosen = jnp.where(above, 1.0, jnp.where(tied, jnp.where(tie_rank < need, 1.0, 0.0), 0.0))
    slot = jnp.dot(chosen.astype(BF16), tri, preferred_element_type=F32)
    aff_ref[0] = aff
    pos_ref[0] = jnp.where(chosen > 0.0, slot, -1.0)


def _router(logits, tri, B, S, cap):
    out = jax.ShapeDtypeStruct((B, N_EXPERTS, S), F32)
    blk = pl.BlockSpec((1, N_EXPERTS, S), lambda b: (b, 0, 0))
    return pl.pallas_call(
        functools.partial(_router_kernel, cap=cap),
        grid=(B,),
        in_specs=[pl.BlockSpec((1, S, LANES), lambda b: (b, 0, 0)), pl.BlockSpec((S, S), lambda b: (0, 0))],
        out_specs=[blk, blk],
        out_shape=[out, out],
        compiler_params=_params(("parallel",)),
        name="router",
    )(logits, tri)


def _moe_kernel(xres_ref, xn_ref, pos_ref, aff_ref, wg_ref, wu_ref, wd_ref, o_ref,
                sel_ref, xe_ref, y_ref, gate_ref, *, S, cap, row_blk):
    b, e, f = pl.program_id(0), pl.program_id(1), pl.program_id(2)

    @pl.when((e == 0) & (f == 0))
    def _():
        pltpu.sync_copy(xres_ref.at[b], o_ref.at[0])

    @pl.when(f == 0)
    def _():
        slot = lax.broadcasted_iota(jnp.int32, (cap, S), 0).astype(F32)
        hit = slot == pos_ref[0, 0]
        sel = jnp.where(hit, 1.0, 0.0).astype(BF16)
        sel_ref[...] = sel
        xe_ref[...] = jnp.dot(sel, xn_ref[0], preferred_element_type=F32).astype(BF16)
        gate = jnp.sum(jnp.where(hit, aff_ref[0, 0], 0.0), axis=1, keepdims=True)
        gate_ref[...] = jnp.broadcast_to(gate, gate_ref.shape)
        y_ref[...] = jnp.zeros_like(y_ref)

    xe = xe_ref[...]
    hid = (_silu(jnp.dot(xe, wg_ref[0], preferred_element_type=F32))
           * jnp.dot(xe, wu_ref[0], preferred_element_type=F32))
    y_ref[...] += jnp.dot(hid.astype(BF16), wd_ref[0], preferred_element_type=F32)

    @pl.when(f == pl.num_programs(2) - 1)
    def _():
        y = (y_ref[...] * gate_ref[:, 0:1]).astype(BF16)
        for r in range(S // row_blk):
            rows = slice(r * row_blk, (r + 1) * row_blk)
            o_ref[0, rows, :] += lax.dot_general(sel_ref[:, rows], y, TN_DIMS, preferred_element_type=F32)


def _moe(xres, xn, pos, aff, wg, wu, wd, B, S, cap, tf):
    D = D_MODEL
    row_blk = min(S, 512)
    return pl.pallas_call(
        functools.partial(_moe_kernel, S=S, cap=cap, row_blk=row_blk),
        grid=(B, N_EXPERTS, D_FF // tf),
        in_specs=[
            pl.BlockSpec(memory_space=pl.ANY),
            pl.BlockSpec((1, S, D), lambda b, e, f: (b, 0, 0)),
            pl.BlockSpec((1, 1, 1, S), lambda b, e, f: (b, e, 0, 0)),
            pl.BlockSpec((1, 1, 1, S), lambda b, e, f: (b, e, 0, 0)),
            pl.BlockSpec((1, D, tf), lambda b, e, f: (e, 0, f)),
            pl.BlockSpec((1, D, tf), lambda b, e, f: (e, 0, f)),
            pl.BlockSpec((1, tf, D), lambda b, e, f: (e, f, 0)),
        ],
        out_specs=pl.BlockSpec((1, S, D), lambda b, e, f: (b, 0, 0)),
        out_shape=jax.ShapeDtypeStruct((B, S, D), F32),
        scratch_shapes=[pltpu.VMEM((cap, S), BF16), pltpu.VMEM((cap, D), BF16),
                        pltpu.VMEM((cap, D), F32), pltpu.VMEM((cap, LANES), F32)],
        compiler_params=_params(("parallel", "arbitrary", "arbitrary")),
        name="moe",
    )(xres, xn, pos, aff, wg, wu, wd)


def _final_norm_kernel(x_ref, g_ref, o_ref):
    o_ref[...] = _rms(x_ref[...], g_ref[...])


def _final_norm(x, g, tm):
    T, D = x.shape
    return pl.pallas_call(
        _final_norm_kernel,
        grid=(T // tm,),
        in_specs=[pl.BlockSpec((tm, D), lambda i: (i, 0)), pl.BlockSpec((1, D), lambda i: (0, 0))],
        out_specs=pl.BlockSpec((tm, D), lambda i: (i, 0)),
        out_shape=jax.ShapeDtypeStruct((T, D), F32),
        compiler_params=_params(("parallel",)),
        name="final_norm",
    )(x, g)


def _rotary_tables(S):
    half = DKB // 2
    freqs = ROPE_BASE ** (-jnp.arange(half, dtype=F32) / half)
    ang = jnp.arange(S, dtype=F32)[:, None] * freqs[None, :]
    cos, sin = jnp.cos(ang), jnp.sin(ang)
    reps = LANES // DKB
    return jnp.tile(cos, (1, 2 * reps)), jnp.tile(jnp.concatenate([-sin, sin], axis=1), (1, reps))


def _pad_cols(a, width):
    return jnp.pad(a, ((0, 0), (0, width - a.shape[1])))


def kernel(x, t5_bias, attn_norm_g, w_in, branch_gate_b, diff_lambda, diff_norm_g, ret_decay_logit, ret_norm_g, mlstm_conv_w, mlstm_gate_b, mlstm_norm_g, w_branch_a, w_branch_b, w_branch_c, w_out, ffn_norm_g, w_router, w_exp_gate, w_exp_up, w_exp_down, final_norm_g):
    B, S, D = x.shape
    assert D == D_MODEL and S % CHUNK == 0
    T = B * S
    depth = w_in.shape[0]
    cap = CAPACITY_FACTOR * S // N_EXPERTS
    tm = min(512, T)
    tq = min(256, S)
    tf = D_FF // 2

    bias = _t5_bias(t5_bias.astype(F32), S, tq)
    cos, sin = _rotary_tables(S)
    tri = (jnp.arange(S)[:, None] < jnp.arange(S)[None, :]).astype(BF16)

    x = x.reshape(T, D).astype(F32)
    for l in range(depth):
        lam_init = 0.8 - 0.6 * math.exp(-0.3 * l)
        (w_aq, w_ak, w_av, w_bq, w_bk, w_bv, w_bg, w_cq, w_ck, w_cv, w_co, w_cgt, w_bgt) = jnp.split(
            w_in[l], _IN_OFFSETS, axis=-1)
        wa_cols = jnp.concatenate([w_aq, w_ak, w_av, w_bv, w_cv], axis=1).astype(BF16)
        wb_cols = jnp.concatenate([w_bgt, w_bq, w_bk, w_bg, w_cq, w_ck, w_co], axis=1).astype(BF16)
        wg_cols = _pad_cols(w_cgt, LANES).astype(BF16)
        g1 = attn_norm_g[l].reshape(1, D)

        za = _norm_matmul(x, g1, wa_cols, BF16, tm, D).reshape(B, S, -1)
        zb = _norm_matmul(x, g1, wb_cols, F32, tm, D)
        zg = _norm_matmul(x, g1, wg_cols, F32, tm, LANES).reshape(B, S, LANES)
        zb3 = zb.reshape(B, S, -1)

        ya = _diff_attention(za, bias, diff_lambda[l].astype(F32), diff_norm_g[l].reshape(1, D), lam_init, B, S, tq)
        dec = jnp.broadcast_to(ret_decay_logit[l].astype(F32).T[:, :, None], (N_HEADS, 2, LANES))
        yb = _retention(za, zb3, dec, cos, sin, ret_norm_g[l].reshape(1, D), B, S)
        gate_b = _pad_cols(mlstm_gate_b[l].astype(F32).reshape(1, GATE_COLS), LANES)
        cw = mlstm_conv_w[l].astype(F32)
        yc = _mlstm(za, zb3, zg, gate_b, cw[:, :D], cw[:, D:], mlstm_norm_g[l].reshape(1, D), B, S)

        x, xn, logits = _merge(
            x, ya.reshape(T, D), yb.reshape(T, D), yc.reshape(T, D), zb, branch_gate_b[l].astype(F32),
            w_branch_a[l].astype(BF16), w_branch_b[l].astype(BF16), w_branch_c[l].astype(BF16),
            w_out[l].astype(BF16), ffn_norm_g[l].reshape(1, D), _pad_cols(w_router[l].astype(F32), LANES), tm // 2)

        aff, pos = _router(logits.reshape(B, S, LANES), tri, B, S, cap)
        x = _moe(x.reshape(B, S, D), xn.reshape(B, S, D), pos.reshape(B, N_EXPERTS, 1, S),
                 aff.reshape(B, N_EXPERTS, 1, S), w_exp_gate[l].astype(BF16), w_exp_up[l].astype(BF16),
                 w_exp_down[l].astype(BF16), B, S, cap, tf).reshape(T, D)
    return _final_norm(x, final_norm_g.reshape(1, D), tm).reshape(B, S, D)


_IN_WIDTHS = (N_HEADS * 2 * DA, N_HEADS * 2 * DA, N_HEADS * HEAD_W, N_HEADS * DKB, N_HEADS * DKB,
              N_HEADS * HEAD_W, N_HEADS * HEAD_W, N_HEADS * HEAD_W, N_HEADS * HEAD_W, N_HEADS * HEAD_W,
              N_HEADS * HEAD_W, GATE_COLS, 3 * D_MODEL)
_IN_OFFSETS = tuple(int(sum(_IN_WIDTHS[:i + 1])) for i in range(len(_IN_WIDTHS) - 1))
```

```python
import functools
import math

import jax
import jax.numpy as jnp
from jax import lax
from jax.experimental import pallas as pl
from jax.experimental.pallas import tpu as pltpu

F32 = jnp.float32
BF16 = jnp.bfloat16

D_MODEL = 1024
N_HEADS = 8
HEAD_W = 128
DA = 64
DKB = 64
CHUNK = 128
CONV_W = 3
N_EXPERTS = 16
D_FF = 2048
CAPACITY_FACTOR = 2
T5_BUCKETS = 32
T5_MAX_DIST = 128
ROPE_BASE = 10000.0
EPS = 1e-6
NEG = -1e30
GATE_COLS = 32
LOG2E = math.log2(math.e)
LANES = 128

VMEM_LIMIT = 56 * 1024 * 1024

NT_DIMS = (((1,), (1,)), ((), ()))
TN_DIMS = (((0,), (0,)), ((), ()))


def _params(sem):
    return pltpu.CompilerParams(dimension_semantics=sem, vmem_limit_bytes=VMEM_LIMIT)


def _log_sigmoid(x):
    return -(jnp.maximum(-x, 0.0) + jnp.log1p(jnp.exp(-jnp.abs(x))))


def _sigmoid(x):
    return 1.0 / (1.0 + jnp.exp(-x))


def _silu(x):
    return x * _sigmoid(x)


def _rms(x, g):
    return x * lax.rsqrt(jnp.mean(x * x, axis=-1, keepdims=True) + EPS) * g


def _layer_norm_head(h, gain):
    h = h - jnp.mean(h, axis=-1, keepdims=True)
    return h * lax.rsqrt(jnp.mean(h * h, axis=-1, keepdims=True) + EPS) * gain


def _norm_matmul_kernel(x_ref, g_ref, w_ref, o_ref, xn_ref):
    @pl.when(pl.program_id(1) == 0)
    def _():
        xn_ref[...] = _rms(x_ref[...], g_ref[...]).astype(BF16)

    o_ref[...] = jnp.dot(xn_ref[...], w_ref[...], preferred_element_type=F32).astype(o_ref.dtype)


def _norm_matmul(x, g, w, out_dtype, tm, tn):
    T, D = x.shape
    N = w.shape[1]
    return pl.pallas_call(
        _norm_matmul_kernel,
        grid=(T // tm, N // tn),
        in_specs=[
            pl.BlockSpec((tm, D), lambda i, j: (i, 0)),
            pl.BlockSpec((1, D), lambda i, j: (0, 0)),
            pl.BlockSpec((D, tn), lambda i, j: (0, j)),
        ],
        out_specs=pl.BlockSpec((tm, tn), lambda i, j: (i, j)),
        out_shape=jax.ShapeDtypeStruct((T, N), out_dtype),
        scratch_shapes=[pltpu.VMEM((tm, D), BF16)],
        compiler_params=_params(("parallel", "arbitrary")),
        name="norm_matmul",
    )(x, g, w)


def _add_norm_matmul_kernel(x_ref, r_ref, g_ref, w_ref, xs_ref, o_ref, xn_ref):
    @pl.when(pl.program_id(1) == 0)
    def _():
        xs = x_ref[...] + r_ref[...]
        xs_ref[...] = xs
        xn_ref[...] = _rms(xs, g_ref[...]).astype(BF16)

    o_ref[...] = jnp.dot(xn_ref[...], w_ref[...], preferred_element_type=F32).astype(o_ref.dtype)


def _add_norm_matmul(x, r, g, w, out_dtype, tm, tn):
    T, D = x.shape
    N = w.shape[1]
    row = pl.BlockSpec((tm, D), lambda i, j: (i, 0))
    return pl.pallas_call(
        _add_norm_matmul_kernel,
        grid=(T // tm, N // tn),
        in_specs=[row, row, pl.BlockSpec((1, D), lambda i, j: (0, 0)), pl.BlockSpec((D, tn), lambda i, j: (0, j))],
        out_specs=[row, pl.BlockSpec((tm, tn), lambda i, j: (i, j))],
        out_shape=[jax.ShapeDtypeStruct((T, D), F32), jax.ShapeDtypeStruct((T, N), out_dtype)],
        scratch_shapes=[pltpu.VMEM((tm, D), BF16)],
        compiler_params=_params(("parallel", "arbitrary")),
        name="add_norm_matmul",
    )(x, r, g, w)


def _t5_bias_kernel(tab_ref, bkt_ref, o_ref, *, tk, S):
    h = pl.program_id(0)
    ki = pl.program_id(1)
    bkt = bkt_ref[...]
    r = jnp.zeros((1, 2 * S), F32)
    for b in range(T5_BUCKETS):
        r = jnp.where(bkt == b, tab_ref[b, h] * LOG2E, r)
    x = jnp.broadcast_to(r, (tk, 2 * S))
    rolled = pltpu.roll(x, ki * tk + S, 1, stride=1, stride_axis=0)
    o_ref[0] = rolled[:, :S]


def _t5_bucket(rel):
    half = T5_BUCKETS // 2
    max_exact = half // 2
    offset = jnp.where(rel > 0, half, 0)
    n = jnp.abs(rel)
    large = max_exact + (jnp.log(jnp.maximum(n, 1).astype(F32) / max_exact)
                         / math.log(T5_MAX_DIST / max_exact) * (half - max_exact)).astype(jnp.int32)
    large = jnp.minimum(large, half - 1)
    return offset + jnp.where(n < max_exact, n, large)


def _t5_bias(table, S, tk):
    rel = S - jnp.arange(2 * S, dtype=jnp.int32)
    bkt = _t5_bucket(rel).astype(jnp.int32).reshape(1, 2 * S)
    return pl.pallas_call(
        functools.partial(_t5_bias_kernel, tk=tk, S=S),
        grid=(N_HEADS, S // tk),
        in_specs=[
            pl.BlockSpec(memory_space=pltpu.SMEM),
            pl.BlockSpec((1, 2 * S), lambda h, k: (0, 0)),
        ],
        out_specs=pl.BlockSpec((1, tk, S), lambda h, k: (h, k, 0)),
        out_shape=jax.ShapeDtypeStruct((N_HEADS, S, S), F32),
        compiler_params=_params(("parallel", "arbitrary")),
        name="t5_bias",
    )(table, bkt)


ONES_ROWS = 16


KEY_BLK = 256


def _attn_kernel(dl_ref, q_ref, k_ref, v_ref, b_ref, g_ref, o_ref, vt_ref, s_ref, *, lam_init):
    @pl.when(pl.program_id(2) == 0)
    def _():
        vt_ref[:HEAD_W, :] = v_ref[0].astype(F32).T.astype(BF16)
        vt_ref[HEAD_W:, :] = jnp.ones((ONES_ROWS, vt_ref.shape[1]), BF16)

    dl = dl_ref[...]
    lam = (jnp.exp(jnp.sum(dl[0:1] * dl[1:2], axis=-1, keepdims=True))
           - jnp.exp(jnp.sum(dl[2:3] * dl[3:4], axis=-1, keepdims=True)) + lam_init)
    q = q_ref[0] * (DA ** -0.5)
    lane = lax.broadcasted_iota(jnp.int32, q.shape, 1)
    q_halves = (jnp.where(lane < DA, q, jnp.zeros_like(q)), jnp.where(lane >= DA, q, jnp.zeros_like(q)))
    n_blk = k_ref.shape[1] // KEY_BLK

    def score_block(half, j):
        rows = slice(j * KEY_BLK, (j + 1) * KEY_BLK)
        s = lax.dot_general(k_ref[0, rows, :], q_halves[half], NT_DIMS, preferred_element_type=F32)
        s = s + b_ref[0, rows, :]
        s_ref[half, rows, :] = s
        return jnp.max(s, axis=0, keepdims=True)

    def value_block(half, j, m):
        rows = slice(j * KEY_BLK, (j + 1) * KEY_BLK)
        e = jnp.exp2(s_ref[half, rows, :] - m).astype(BF16)
        return jnp.dot(vt_ref[:, rows], e, preferred_element_type=F32)

    m1 = functools.reduce(jnp.maximum, [score_block(0, j) for j in range(n_blk)])
    m2, acc1 = None, None
    for j in range(n_blk):
        mj = score_block(1, j)
        m2 = mj if m2 is None else jnp.maximum(m2, mj)
        p = value_block(0, j, m1)
        acc1 = p if acc1 is None else acc1 + p
    acc2 = functools.reduce(lambda a, b: a + b, [value_block(1, j, m2) for j in range(n_blk)])

    o = (acc1[:HEAD_W] / acc1[HEAD_W:HEAD_W + 1]
         - lam * (acc2[:HEAD_W] / acc2[HEAD_W:HEAD_W + 1]))
    y = o * lax.rsqrt(jnp.mean(o * o, axis=0, keepdims=True) + EPS) * g_ref[...] * (1.0 - lam_init)
    o_ref[0] = y.T.astype(o_ref.dtype)


def _diff_attention(za, bias_t, dl, gain_col, lam_init, B, S, tq):
    return pl.pallas_call(
        functools.partial(_attn_kernel, lam_init=lam_init),
        grid=(N_HEADS, B, S // tq),
        in_specs=[
            pl.BlockSpec((4, DA), lambda h, b, q: (0, 0)),
            pl.BlockSpec((1, tq, HEAD_W), lambda h, b, q: (b, q, h)),
            pl.BlockSpec((1, S, HEAD_W), lambda h, b, q: (b, 0, N_HEADS + h)),
            pl.BlockSpec((1, S, HEAD_W), lambda h, b, q: (b, 0, 2 * N_HEADS + h)),
            pl.BlockSpec((1, S, tq), lambda h, b, q: (h, 0, q)),
            pl.BlockSpec((HEAD_W, 1), lambda h, b, q: (h, 0)),
        ],
        out_specs=pl.BlockSpec((1, tq, HEAD_W), lambda h, b, q: (b, q, h)),
        out_shape=jax.ShapeDtypeStruct((B, S, D_MODEL), BF16),
        scratch_shapes=[pltpu.VMEM((HEAD_W + ONES_ROWS, S), BF16), pltpu.VMEM((2, S, tq), F32)],
        compiler_params=_params(("parallel", "parallel", "arbitrary")),
        name="diff_attention",
    )(dl, za, za, za, bias_t, gain_col)


def _retention_kernel(dec_ref, q_ref, k_ref, v_ref, bg_ref, cos_ref, sin_ref, g_ref, o_ref,
                      qs_ref, kt_ref, acc_ref, kvf_ref, kvb_ref, r_ref, *, S):
    L = CHUNK
    n = S // L
    lane = lax.broadcasted_iota(jnp.int32, (S, LANES), 1)
    first_half = (lane & (DKB // 2)) == 0
    cos = cos_ref[...]
    sin = sin_ref[...]

    def rotary(t):
        partner = jnp.where(first_half, pltpu.roll(t, LANES - DKB // 2, 1), pltpu.roll(t, DKB // 2, 1))
        return t * cos + partner * sin

    qs_ref[...] = rotary(q_ref[0])
    ks = rotary(k_ref[0]) * (DKB ** -0.5)
    for c in range(n):
        kt_ref[c] = ks[c * L:(c + 1) * L].T

    row = lax.broadcasted_iota(jnp.int32, (L, LANES), 0).astype(F32)
    col = lax.broadcasted_iota(jnp.int32, (L, LANES), 1).astype(F32)
    lane_l = lax.broadcasted_iota(jnp.int32, (L, LANES), 1)
    pos_row = col[0:1]
    diff = row - col
    heads = []
    for j in range(2):
        lg = _log_sigmoid(dec_ref[j])
        lgf, lgb = lg[0:1], lg[1:2]
        heads.append(dict(
            mask=(lane_l // DKB) == j,
            dmat=(jnp.where(diff >= 0, jnp.exp(lgf * jnp.maximum(diff, 0.0)), 0.0)
                  + jnp.where(diff <= 0, jnp.exp(lgb * jnp.maximum(-diff, 0.0)), 0.0)),
            xi_f=jnp.exp(lgf * (row + 1.0)), xi_b=jnp.exp(lgb * (L - row)),
            zeta_f=jnp.exp(lgf * (L - 1.0 - pos_row)), zeta_b=jnp.exp(lgb * pos_row),
            decay_f=jnp.exp(lgf * L), decay_b=jnp.exp(lgb * L)))

    def prepare(c, carry):
        sl = pl.ds(pl.multiple_of(c * L, L), L)
        qc = qs_ref[sl, :]
        kt = kt_ref[c]
        ktb = kt.astype(BF16)
        for j, hd in enumerate(heads):
            vj = v_ref[0, sl, j * HEAD_W:(j + 1) * HEAD_W]
            qj = jnp.where(hd["mask"], qc, 0.0).astype(BF16)
            inner = jnp.dot(qj, ktb, preferred_element_type=F32) * hd["dmat"]
            acc_ref[j, sl, :] = jnp.dot(inner.astype(BF16), vj, preferred_element_type=F32)
            kvf_ref[j, c] = jnp.dot((kt * hd["zeta_f"]).astype(BF16), vj, preferred_element_type=F32)
            kvb_ref[j, c] = jnp.dot((kt * hd["zeta_b"]).astype(BF16), vj, preferred_element_type=F32)
        return carry

    lax.fori_loop(0, n, prepare, 0, unroll=2)
    r_ref[...] = jnp.zeros_like(r_ref)

    def recur(i, carry):
        for j, hd in enumerate(heads):
            for rev in (False, True):
                c = (n - 1 - i) if rev else i
                sl = pl.ds(pl.multiple_of(c * L, L), L)
                xi, decay, kv_ref = ((hd["xi_b"], hd["decay_b"], kvb_ref) if rev
                                     else (hd["xi_f"], hd["decay_f"], kvf_ref))
                r = r_ref[2 * j + rev]
                qj = (jnp.where(hd["mask"], qs_ref[sl, :], 0.0) * xi).astype(BF16)
                acc_ref[j, sl, :] += jnp.dot(qj, r.astype(BF16), preferred_element_type=F32)
                r_ref[2 * j + rev] = decay * r + kv_ref[j, c]
        return carry

    lax.fori_loop(0, n, recur, 0, unroll=2)
    for j in range(2):
        cols = slice(j * HEAD_W, (j + 1) * HEAD_W)
        y = _silu(bg_ref[0, :, cols]) * _layer_norm_head(acc_ref[j], g_ref[:, cols])
        o_ref[0, :, cols] = y.astype(o_ref.dtype)


def _retention(za, zb, dec, cos, sin, gain, B, S):
    qk_blk = 3 * N_HEADS
    pair_w = 2 * HEAD_W
    n = S // CHUNK
    return pl.pallas_call(
        functools.partial(_retention_kernel, S=S),
        grid=(B, N_HEADS // 2),
        in_specs=[
            pl.BlockSpec((2, 2, LANES), lambda b, p: (p, 0, 0)),
            pl.BlockSpec((1, S, LANES), lambda b, p: (b, 0, qk_blk + p)),
            pl.BlockSpec((1, S, LANES), lambda b, p: (b, 0, qk_blk + N_HEADS // 2 + p)),
            pl.BlockSpec((1, S, pair_w), lambda b, p: (b, 0, 3 * N_HEADS // 2 + p)),
            pl.BlockSpec((1, S, pair_w), lambda b, p: (b, 0, 4 * N_HEADS // 2 + p)),
            pl.BlockSpec((S, LANES), lambda b, p: (0, 0)),
            pl.BlockSpec((S, LANES), lambda b, p: (0, 0)),
            pl.BlockSpec((1, pair_w), lambda b, p: (0, p)),
        ],
        out_specs=pl.BlockSpec((1, S, pair_w), lambda b, p: (b, 0, p)),
        out_shape=jax.ShapeDtypeStruct((B, S, D_MODEL), BF16),
        scratch_shapes=[pltpu.VMEM((S, LANES), F32), pltpu.VMEM((n, LANES, CHUNK), F32),
                        pltpu.VMEM((2, S, HEAD_W), F32),
                        pltpu.VMEM((2, n, LANES, HEAD_W), F32), pltpu.VMEM((2, n, LANES, HEAD_W), F32),
                        pltpu.VMEM((4, LANES, HEAD_W), F32)],
        compiler_params=_params(("parallel", "arbitrary")),
        name="retention",
    )(dec, zb, zb, za, zb, cos, sin, gain)


def _mlstm_kernel(q_ref, k_ref, v_ref, gt_ref, gb_ref, cwq_ref, cwk_ref, co_ref, gn_ref, o_ref,
                  qs_ref, kt_ref, gsc_ref, va_ref, sv_ref, kv_ref, bcb_ref, dmx_ref, sc_ref, hf_ref, hb_ref,
                  c_ref, n_ref, *, S):
    L = CHUNK
    n = S // L
    h = pl.program_id(1)

    rows = lax.broadcasted_iota(jnp.int32, (S, HEAD_W), 0)

    def conv_silu(u, w):
        prev = jnp.where(rows == 0, 0.0, pltpu.roll(u, 1, 0))
        nxt = jnp.where(rows == S - 1, 0.0, pltpu.roll(u, S - 1, 0))
        return _silu(w[0:1] * prev + w[1:2] * u + w[2:3] * nxt)

    qs_ref[...] = conv_silu(q_ref[0], cwq_ref[...]).astype(BF16)
    va_ref[:, :HEAD_W] = v_ref[0]
    va_ref[:, HEAD_W:] = jnp.ones_like(v_ref[0])

    ks = conv_silu(k_ref[0], cwk_ref[...]) * (HEAD_W ** -0.5)
    gates = gt_ref[0] + gb_ref[...]
    gsub = lax.broadcasted_iota(jnp.int32, (GATE_COLS, L), 0)
    for c in range(n):
        kt_ref[c] = ks[c * L:(c + 1) * L].T
        gt = gates[c * L:(c + 1) * L].T[:GATE_COLS]
        pick = lambda j: jnp.sum(jnp.where(gsub == j, gt, 0.0), axis=0, keepdims=True)
        gsc_ref[c] = jnp.concatenate(
            [pick(h), _log_sigmoid(pick(N_HEADS + h)), pick(2 * N_HEADS + h),
             _log_sigmoid(pick(3 * N_HEADS + h)), jnp.zeros((4, L), F32)], axis=0)

    ri = lax.broadcasted_iota(jnp.int32, (L, L), 0)
    ci = lax.broadcasted_iota(jnp.int32, (L, L), 1)
    eye = ri == ci
    ones_rhs = jnp.ones((2 * L, LANES), BF16)

    def prepare(c, carry):
        sl = pl.ds(pl.multiple_of(c * L, L), L)
        kt = kt_ref[c]
        g = gsc_ref[c]
        va = va_ref[sl, :]
        qk = jnp.dot(qs_ref[sl, :], kt.astype(BF16), preferred_element_type=F32)
        for d in range(2):
            causal = (ci >= ri) if d else (ci <= ri)
            ii = g[2 * d:2 * d + 1]
            lf = g[2 * d + 1:2 * d + 2]
            lfm = jnp.where(causal, lf, 0.0)
            hi = lfm.astype(BF16)
            lo = (lfm - hi.astype(F32)).astype(BF16)
            b_cb = jnp.dot(jnp.concatenate([hi, lo], axis=1), ones_rhs, preferred_element_type=F32)
            b_row = jnp.sum(jnp.where(eye, b_cb, 0.0), axis=0, keepdims=True)
            b_last = b_cb[0:1] if d else b_cb[L - 1:L]
            r_row = ii - b_row
            dlog = jnp.where(causal, b_cb + r_row, NEG)
            dmax = jnp.max(dlog, axis=1, keepdims=True)
            s0 = (qk * jnp.exp(dlog - dmax)).astype(BF16)
            sv_ref[d, c] = jnp.dot(s0, va, preferred_element_type=F32)
            wlog = r_row + b_last[:, 0:1]
            wmax = jnp.max(wlog, axis=1, keepdims=True)
            w0 = jnp.exp(wlog - wmax)
            kv_ref[d, c] = jnp.dot((kt * w0).astype(BF16), va, preferred_element_type=F32)
            bcb_ref[d, c] = b_cb
            dmx_ref[d, c] = jnp.broadcast_to(dmax, (L, LANES))
            sc_ref[d, c] = jnp.concatenate(
                [b_last, jnp.broadcast_to(wmax, (1, LANES)), jnp.zeros((6, LANES), F32)], axis=0)
        return carry

    lax.fori_loop(0, n, prepare, 0, unroll=2)

    c_ref[...] = jnp.zeros_like(c_ref)
    n_ref[...] = jnp.zeros_like(n_ref)

    def recur(i, ms):
        out = []
        for d in range(2):
            c = (n - 1 - i) if d else i
            m = ms[d]
            sl = pl.ds(pl.multiple_of(c * L, L), L)
            q = qs_ref[sl, :]
            b_cb, dmax, sv, kv, sc = bcb_ref[d, c], dmx_ref[d, c], sv_ref[d, c], kv_ref[d, c], sc_ref[d, c]
            b_last, wmax = sc[0:1], sc[1:2]
            inter = b_cb + m
            m_row = jnp.maximum(dmax, inter)
            a = jnp.exp(dmax - m_row)
            w_state = jnp.exp(inter - m_row)
            qc = jnp.dot(q, c_ref[d].astype(BF16), preferred_element_type=F32)
            qn = jnp.dot(q, n_ref[d].astype(BF16), preferred_element_type=F32)
            num = a * sv[:, :HEAD_W] + w_state * qc
            den = a * sv[:, HEAD_W:] + w_state * qn
            hh = num / jnp.maximum(jnp.abs(den), jnp.exp(-m_row))
            if d:
                hb_ref[sl, :] = hh
            else:
                hf_ref[sl, :] = hh
            m_new = jnp.maximum(b_last + m, wmax)
            beta = jnp.exp(wmax - m_new)
            decay = jnp.exp(b_last + m - m_new)
            c_ref[d] = decay * c_ref[d] + beta * kv[:, :HEAD_W]
            n_ref[d] = decay * n_ref[d] + beta * kv[:, HEAD_W:]
            out.append(m_new)
        return tuple(out)

    m0 = jnp.full((1, LANES), NEG, F32)
    lax.fori_loop(0, n, recur, (m0, m0), unroll=2)

    y = _sigmoid(co_ref[0]) * _layer_norm_head(hf_ref[...] + hb_ref[...], gn_ref[...])
    o_ref[0] = y.astype(o_ref.dtype)


def _mlstm(za, zb, zg, gate_b, cwq, cwk, gain, B, S):
    n = S // CHUNK
    per_chunk = lambda rows, cols: pltpu.VMEM((2, n, rows, cols), F32)
    return pl.pallas_call(
        functools.partial(_mlstm_kernel, S=S),
        grid=(B, N_HEADS),
        in_specs=[
            pl.BlockSpec((1, S, HEAD_W), lambda b, h: (b, 0, 5 * N_HEADS + h)),
            pl.BlockSpec((1, S, HEAD_W), lambda b, h: (b, 0, 6 * N_HEADS + h)),
            pl.BlockSpec((1, S, HEAD_W), lambda b, h: (b, 0, 4 * N_HEADS + h)),
            pl.BlockSpec((1, S, LANES), lambda b, h: (b, 0, 0)),
            pl.BlockSpec((1, LANES), lambda b, h: (0, 0)),
            pl.BlockSpec((CONV_W, HEAD_W), lambda b, h: (0, h)),
            pl.BlockSpec((CONV_W, HEAD_W), lambda b, h: (0, h)),
            pl.BlockSpec((1, S, HEAD_W), lambda b, h: (b, 0, 7 * N_HEADS + h)),
            pl.BlockSpec((1, HEAD_W), lambda b, h: (0, h)),
        ],
        out_specs=pl.BlockSpec((1, S, HEAD_W), lambda b, h: (b, 0, h)),
        out_shape=jax.ShapeDtypeStruct((B, S, D_MODEL), BF16),
        scratch_shapes=[
            pltpu.VMEM((S, HEAD_W), BF16),
            pltpu.VMEM((n, HEAD_W, CHUNK), F32),
            pltpu.VMEM((n, 8, CHUNK), F32),
            pltpu.VMEM((S, 2 * HEAD_W), BF16),
            per_chunk(CHUNK, 2 * HEAD_W),
            per_chunk(HEAD_W, 2 * HEAD_W),
            per_chunk(CHUNK, LANES),
            per_chunk(CHUNK, LANES),
            per_chunk(8, LANES),
            pltpu.VMEM((S, HEAD_W), F32), pltpu.VMEM((S, HEAD_W), F32),
            pltpu.VMEM((2, HEAD_W, HEAD_W), F32), pltpu.VMEM((2, HEAD_W, HEAD_W), F32),
        ],
        compiler_params=_params(("parallel", "arbitrary")),
        name="mlstm",
    )(zb, zb, za, zg, gate_b, cwq, cwk, zb, gain)


def _merge_kernel(x_ref, ya_ref, yb_ref, yc_ref, bgt_ref, gb_ref, wa_ref, wb_ref, wc_ref, wo_ref,
                  g2_ref, wr_ref, xo_ref, xn_ref, lg_ref):
    D = D_MODEL
    gate = lambda j: _sigmoid(bgt_ref[:, j * D:(j + 1) * D] + gb_ref[j:j + 1, :])
    mixed = (gate(0) * jnp.dot(ya_ref[...], wa_ref[...], preferred_element_type=F32)
             + gate(1) * jnp.dot(yb_ref[...], wb_ref[...], preferred_element_type=F32)
             + gate(2) * jnp.dot(yc_ref[...], wc_ref[...], preferred_element_type=F32))
    x = x_ref[...] + jnp.dot(mixed.astype(BF16), wo_ref[...], preferred_element_type=F32)
    xo_ref[...] = x
    xn = _rms(x, g2_ref[...])
    xn_ref[...] = xn.astype(BF16)
    lg_ref[...] = jnp.dot(xn, wr_ref[...], preferred_element_type=F32, precision=lax.Precision.HIGHEST)


def _merge(x, ya, yb, yc, zb, gate_b, wa, wb, wc, wo, g2, wr, tm):
    T, D = x.shape
    row = lambda i: (i, 0)
    const = lambda i: (0, 0)
    return pl.pallas_call(
        _merge_kernel,
        grid=(T // tm,),
        in_specs=[
            pl.BlockSpec((tm, D), row), pl.BlockSpec((tm, D), row), pl.BlockSpec((tm, D), row),
            pl.BlockSpec((tm, D), row), pl.BlockSpec((tm, 3 * D), row), pl.BlockSpec((3, D), const),
            pl.BlockSpec((D, D), const), pl.BlockSpec((D, D), const), pl.BlockSpec((D, D), const),
            pl.BlockSpec((D, D), const), pl.BlockSpec((1, D), const), pl.BlockSpec((D, LANES), const),
        ],
        out_specs=[pl.BlockSpec((tm, D), row), pl.BlockSpec((tm, D), row), pl.BlockSpec((tm, LANES), row)],
        out_shape=[jax.ShapeDtypeStruct((T, D), F32), jax.ShapeDtypeStruct((T, D), BF16),
                   jax.ShapeDtypeStruct((T, LANES), F32)],
        compiler_params=_params(("parallel",)),
        name="merge",
    )(x, ya, yb, yc, zb, gate_b, wa, wb, wc, wo, g2, wr)


def _router_kernel(lg_ref, tri_ref, aff_ref, pos_ref, *, cap):
    E = N_EXPERTS
    logits = lg_ref[0].T[:E]
    e = jnp.exp(logits - jnp.max(logits, axis=0, keepdims=True))
    aff = e / jnp.sum(e, axis=0, keepdims=True)
    key = pltpu.bitcast(aff, jnp.int32)

    def refine(i, thr):
        cand = thr | (jnp.int32(1) << (30 - i))
        cnt = jnp.sum(jnp.where(key >= cand, 1.0, 0.0), axis=1, keepdims=True)
        return jnp.where(cnt >= cap, cand, thr)

    thr = lax.fori_loop(0, 31, refine, jnp.zeros((E, 1), jnp.int32))
    above = key > thr
    tied = key == thr
    need = cap - jnp.sum(jnp.where(above, 1.0, 0.0), axis=1, keepdims=True)
    tri = tri_ref[...]
    tie_rank = jnp.dot(jnp.where(tied, 1.0, 0.0).astype(BF16), tri, preferred_element_type=F32)
    chosen = jnp.where(above, 1.0, jnp.where(tied, jnp.where(tie_rank < need, 1.0, 0.0), 0.0))
    slot = jnp.dot(chosen.astype(BF16), tri, preferred_element_type=F32)
    aff_ref[0] = aff
    pos_ref[0] = jnp.where(chosen > 0.0, slot, -1.0)


def _router(logits, tri, B, S, cap):
    out = jax.ShapeDtypeStruct((B, N_EXPERTS, S), F32)
    blk = pl.BlockSpec((1, N_EXPERTS, S), lambda b: (b, 0, 0))
    return pl.pallas_call(
        functools.partial(_router_kernel, cap=cap),
        grid=(B,),
        in_specs=[pl.BlockSpec((1, S, LANES), lambda b: (b, 0, 0)), pl.BlockSpec((S, S), lambda b: (0, 0))],
        out_specs=[blk, blk],
        out_shape=[out, out],
        compiler_params=_params(("parallel",)),
        name="router",
    )(logits, tri)


def _moe_kernel(xn_ref, pos_ref, aff_ref, wg_ref, wu_ref, wd_ref, o_ref,
                sel_ref, xe_ref, y_ref, gate_ref, *, S, cap, row_blk):
    e, f = pl.program_id(1), pl.program_id(2)

    @pl.when((e == 0) & (f == 0))
    def _():
        o_ref[...] = jnp.zeros_like(o_ref)

    @pl.when(f == 0)
    def _():
        slot = lax.broadcasted_iota(jnp.int32, (cap, S), 0).astype(F32)
        hit = slot == pos_ref[0, 0]
        sel = jnp.where(hit, 1.0, 0.0).astype(BF16)
        sel_ref[...] = sel
        xe_ref[...] = jnp.dot(sel, xn_ref[0], preferred_element_type=F32).astype(BF16)
        gate = jnp.sum(jnp.where(hit, aff_ref[0, 0], 0.0), axis=1, keepdims=True)
        gate_ref[...] = jnp.broadcast_to(gate, gate_ref.shape)
        y_ref[...] = jnp.zeros_like(y_ref)

    xe = xe_ref[...]
    hid = (_silu(jnp.dot(xe, wg_ref[0], preferred_element_type=F32))
           * jnp.dot(xe, wu_ref[0], preferred_element_type=F32))
    y_ref[...] += jnp.dot(hid.astype(BF16), wd_ref[0], preferred_element_type=F32)

    @pl.when(f == pl.num_programs(2) - 1)
    def _():
        y = (y_ref[...] * gate_ref[:, 0:1]).astype(BF16)
        for r in range(S // row_blk):
            rows = slice(r * row_blk, (r + 1) * row_blk)
            o_ref[0, rows, :] += lax.dot_general(sel_ref[:, rows], y, TN_DIMS, preferred_element_type=F32)


def _moe(xn, pos, aff, wg, wu, wd, B, S, cap, tf):
    D = D_MODEL
    row_blk = min(S, 512)
    return pl.pallas_call(
        functools.partial(_moe_kernel, S=S, cap=cap, row_blk=row_blk),
        grid=(B, N_EXPERTS, D_FF // tf),
        in_specs=[
            pl.BlockSpec((1, S, D), lambda b, e, f: (b, 0, 0)),
            pl.BlockSpec((1, 1, 1, S), lambda b, e, f: (b, e, 0, 0)),
            pl.BlockSpec((1, 1, 1, S), lambda b, e, f: (b, e, 0, 0)),
            pl.BlockSpec((1, D, tf), lambda b, e, f: (e, 0, f)),
            pl.BlockSpec((1, D, tf), lambda b, e, f: (e, 0, f)),
            pl.BlockSpec((1, tf, D), lambda b, e, f: (e, f, 0)),
        ],
        out_specs=pl.BlockSpec((1, S, D), lambda b, e, f: (b, 0, 0)),
        out_shape=jax.ShapeDtypeStruct((B, S, D), F32),
        scratch_shapes=[pltpu.VMEM((cap, S), BF16), pltpu.VMEM((cap, D), BF16),
                        pltpu.VMEM((cap, D), F32), pltpu.VMEM((cap, LANES), F32)],
        compiler_params=_params(("parallel", "arbitrary", "arbitrary")),
        name="moe",
    )(xn, pos, aff, wg, wu, wd)


def _final_norm_kernel(x_ref, r_ref, g_ref, o_ref):
    o_ref[...] = _rms(x_ref[...] + r_ref[...], g_ref[...])


def _final_norm(x, r, g, tm):
    T, D = x.shape
    row = pl.BlockSpec((tm, D), lambda i: (i, 0))
    return pl.pallas_call(
        _final_norm_kernel,
        grid=(T // tm,),
        in_specs=[row, row, pl.BlockSpec((1, D), lambda i: (0, 0))],
        out_specs=row,
        out_shape=jax.ShapeDtypeStruct((T, D), F32),
        compiler_params=_params(("parallel",)),
        name="final_norm",
    )(x, r, g)


def _rotary_tables(S):
    half = DKB // 2
    freqs = ROPE_BASE ** (-jnp.arange(half, dtype=F32) / half)
    ang = jnp.arange(S, dtype=F32)[:, None] * freqs[None, :]
    cos, sin = jnp.cos(ang), jnp.sin(ang)
    reps = LANES // DKB
    return jnp.tile(cos, (1, 2 * reps)), jnp.tile(jnp.concatenate([-sin, sin], axis=1), (1, reps))


def _pad_cols(a, width):
    return jnp.pad(a, ((0, 0), (0, width - a.shape[1])))


def kernel(x, t5_bias, attn_norm_g, w_in, branch_gate_b, diff_lambda, diff_norm_g, ret_decay_logit, ret_norm_g, mlstm_conv_w, mlstm_gate_b, mlstm_norm_g, w_branch_a, w_branch_b, w_branch_c, w_out, ffn_norm_g, w_router, w_exp_gate, w_exp_up, w_exp_down, final_norm_g):
    B, S, D = x.shape
    assert D == D_MODEL and S % CHUNK == 0
    T = B * S
    depth = w_in.shape[0]
    cap = CAPACITY_FACTOR * S // N_EXPERTS
    tm = min(1024, T)
    tq = min(512, S)
    tf = D_FF // 2

    bias = _t5_bias(t5_bias.astype(F32), S, min(256, S))
    cos, sin = _rotary_tables(S)
    tri = (jnp.arange(S)[:, None] < jnp.arange(S)[None, :]).astype(BF16)

    x = x.reshape(T, D).astype(F32)
    ffn = None
    for l in range(depth):
        lam_init = 0.8 - 0.6 * math.exp(-0.3 * l)
        (w_aq, w_ak, w_av, w_bq, w_bk, w_bv, w_bg, w_cq, w_ck, w_cv, w_co, w_cgt, w_bgt) = jnp.split(
            w_in[l], _IN_OFFSETS, axis=-1)
        wa_cols = jnp.concatenate([w_aq * LOG2E, w_ak, w_av, w_bv, w_cv], axis=1).astype(BF16)
        wb_cols = jnp.concatenate([w_bgt, w_bq, w_bk, w_bg, w_cq, w_ck, w_co], axis=1).astype(BF16)
        wg_cols = _pad_cols(w_cgt, LANES).astype(BF16)
        g1 = attn_norm_g[l].reshape(1, D)

        if ffn is None:
            za = _norm_matmul(x, g1, wa_cols, BF16, tm, D)
        else:
            x, za = _add_norm_matmul(x, ffn, g1, wa_cols, BF16, tm, D)
        za = za.reshape(B, S, -1)
        zb = _norm_matmul(x, g1, wb_cols, F32, tm, D)
        zg = _norm_matmul(x, g1, wg_cols, F32, tm, LANES).reshape(B, S, LANES)
        zb3 = zb.reshape(B, S, -1)

        ya = _diff_attention(za, bias, diff_lambda[l].astype(F32), diff_norm_g[l].reshape(D, 1), lam_init, B, S, tq)
        dec = jnp.broadcast_to(ret_decay_logit[l].astype(F32).T[:, :, None], (N_HEADS, 2, LANES))
        yb = _retention(za, zb3, dec, cos, sin, ret_norm_g[l].reshape(1, D), B, S)
        gate_b = _pad_cols(mlstm_gate_b[l].astype(F32).reshape(1, GATE_COLS), LANES)
        cw = mlstm_conv_w[l].astype(F32)
        yc = _mlstm(za, zb3, zg, gate_b, cw[:, :D], cw[:, D:], mlstm_norm_g[l].reshape(1, D), B, S)

        x, xn, logits = _merge(
            x, ya.reshape(T, D), yb.reshape(T, D), yc.reshape(T, D), zb, branch_gate_b[l].astype(F32),
            w_branch_a[l].astype(BF16), w_branch_b[l].astype(BF16), w_branch_c[l].astype(BF16),
            w_out[l].astype(BF16), ffn_norm_g[l].reshape(1, D), _pad_cols(w_router[l].astype(F32), LANES),
            min(256, T))

        aff, pos = _router(logits.reshape(B, S, LANES), tri, B, S, cap)
        ffn = _moe(xn.reshape(B, S, D), pos.reshape(B, N_EXPERTS, 1, S),
                   aff.reshape(B, N_EXPERTS, 1, S), w_exp_gate[l].astype(BF16), w_exp_up[l].astype(BF16),
                   w_exp_down[l].astype(BF16), B, S, cap, tf).reshape(T, D)
    return _final_norm(x, ffn, final_norm_g.reshape(1, D), tm).reshape(B, S, D)


_IN_WIDTHS = (N_HEADS * 2 * DA, N_HEADS * 2 * DA, N_HEADS * HEAD_W, N_HEADS * DKB, N_HEADS * DKB,
              N_HEADS * HEAD_W, N_HEADS * HEAD_W, N_HEADS * HEAD_W, N_HEADS * HEAD_W, N_HEADS * HEAD_W,
              N_HEADS * HEAD_W, GATE_COLS, 3 * D_MODEL)
_IN_OFFSETS = tuple(int(sum(_IN_WIDTHS[:i + 1])) for i in range(len(_IN_WIDTHS) - 1))
```

```python
import functools
import math

import jax
import jax.numpy as jnp
from jax import lax
from jax.experimental import pallas as pl
from jax.experimental.pallas import tpu as pltpu

F32 = jnp.float32
BF16 = jnp.bfloat16

D_MODEL = 1024
N_HEADS = 8
HEAD_W = 128
DA = 64
DKB = 64
CHUNK = 128
CONV_W = 3
N_EXPERTS = 16
D_FF = 2048
CAPACITY_FACTOR = 2
T5_BUCKETS = 32
T5_MAX_DIST = 128
ROPE_BASE = 10000.0
EPS = 1e-6
NEG = -1e30
GATE_COLS = 32
LOG2E = math.log2(math.e)
LANES = 128

VMEM_LIMIT = 56 * 1024 * 1024

NT_DIMS = (((1,), (1,)), ((), ()))
TN_DIMS = (((0,), (0,)), ((), ()))


def _params(sem):
    return pltpu.CompilerParams(dimension_semantics=sem, vmem_limit_bytes=VMEM_LIMIT)


def _log_sigmoid(x):
    return -(jnp.maximum(-x, 0.0) + jnp.log1p(jnp.exp(-jnp.abs(x))))


def _sigmoid(x):
    return 1.0 / (1.0 + jnp.exp(-x))


def _silu(x):
    return x * _sigmoid(x)


def _rms(x, g):
    return x * lax.rsqrt(jnp.mean(x * x, axis=-1, keepdims=True) + EPS) * g


def _layer_norm_head(h, gain):
    h = h - jnp.mean(h, axis=-1, keepdims=True)
    return h * lax.rsqrt(jnp.mean(h * h, axis=-1, keepdims=True) + EPS) * gain


def _norm_matmul_kernel(x_ref, g_ref, w_ref, o_ref, xn_ref):
    @pl.when(pl.program_id(1) == 0)
    def _():
        xn_ref[...] = _rms(x_ref[...], g_ref[...]).astype(BF16)

    o_ref[...] = jnp.dot(xn_ref[...], w_ref[...], preferred_element_type=F32).astype(o_ref.dtype)


def _norm_matmul(x, g, w, out_dtype, tm, tn):
    T, D = x.shape
    N = w.shape[1]
    return pl.pallas_call(
        _norm_matmul_kernel,
        grid=(T // tm, N // tn),
        in_specs=[
            pl.BlockSpec((tm, D), lambda i, j: (i, 0)),
            pl.BlockSpec((1, D), lambda i, j: (0, 0)),
            pl.BlockSpec((D, tn), lambda i, j: (0, j)),
        ],
        out_specs=pl.BlockSpec((tm, tn), lambda i, j: (i, j)),
        out_shape=jax.ShapeDtypeStruct((T, N), out_dtype),
        scratch_shapes=[pltpu.VMEM((tm, D), BF16)],
        compiler_params=_params(("parallel", "arbitrary")),
        name="norm_matmul",
    )(x, g, w)


def _add_norm_matmul_kernel(x_ref, r_ref, g_ref, w_ref, xs_ref, o_ref, xn_ref):
    @pl.when(pl.program_id(1) == 0)
    def _():
        xs = x_ref[...] + r_ref[...]
        xs_ref[...] = xs
        xn_ref[...] = _rms(xs, g_ref[...]).astype(BF16)

    o_ref[...] = jnp.dot(xn_ref[...], w_ref[...], preferred_element_type=F32).astype(o_ref.dtype)


def _add_norm_matmul(x, r, g, w, out_dtype, tm, tn):
    T, D = x.shape
    N = w.shape[1]
    row = pl.BlockSpec((tm, D), lambda i, j: (i, 0))
    return pl.pallas_call(
        _add_norm_matmul_kernel,
        grid=(T // tm, N // tn),
        in_specs=[row, row, pl.BlockSpec((1, D), lambda i, j: (0, 0)), pl.BlockSpec((D, tn), lambda i, j: (0, j))],
        out_specs=[row, pl.BlockSpec((tm, tn), lambda i, j: (i, j))],
        out_shape=[jax.ShapeDtypeStruct((T, D), F32), jax.ShapeDtypeStruct((T, N), out_dtype)],
        scratch_shapes=[pltpu.VMEM((tm, D), BF16)],
        compiler_params=_params(("parallel", "arbitrary")),
        name="add_norm_matmul",
    )(x, r, g, w)


def _t5_bias_kernel(tab_ref, bkt_ref, o_ref, *, tk, S):
    h = pl.program_id(0)
    ki = pl.program_id(1)
    bkt = bkt_ref[...]
    r = jnp.zeros((1, 2 * S), F32)
    for b in range(T5_BUCKETS):
        r = jnp.where(bkt == b, tab_ref[b, h] * LOG2E, r)
    x = jnp.broadcast_to(r, (tk, 2 * S))
    rolled = pltpu.roll(x, ki * tk + S, 1, stride=1, stride_axis=0)
    o_ref[0] = rolled[:, :S]


def _t5_bucket(rel):
    half = T5_BUCKETS // 2
    max_exact = half // 2
    offset = jnp.where(rel > 0, half, 0)
    n = jnp.abs(rel)
    large = max_exact + (jnp.log(jnp.maximum(n, 1).astype(F32) / max_exact)
                         / math.log(T5_MAX_DIST / max_exact) * (half - max_exact)).astype(jnp.int32)
    large = jnp.minimum(large, half - 1)
    return offset + jnp.where(n < max_exact, n, large)


def _t5_bias(table, S, tk):
    rel = S - jnp.arange(2 * S, dtype=jnp.int32)
    bkt = _t5_bucket(rel).astype(jnp.int32).reshape(1, 2 * S)
    return pl.pallas_call(
        functools.partial(_t5_bias_kernel, tk=tk, S=S),
        grid=(N_HEADS, S // tk),
        in_specs=[
            pl.BlockSpec(memory_space=pltpu.SMEM),
            pl.BlockSpec((1, 2 * S), lambda h, k: (0, 0)),
        ],
        out_specs=pl.BlockSpec((1, tk, S), lambda h, k: (h, k, 0)),
        out_shape=jax.ShapeDtypeStruct((N_HEADS, S, S), F32),
        compiler_params=_params(("parallel", "arbitrary")),
        name="t5_bias",
    )(table, bkt)


ONES_ROWS = 16


KEY_BLK = 256


def _attn_kernel(dl_ref, q_ref, k_ref, v_ref, b_ref, g_ref, o_ref, vt_ref, *, lam_init):
    @pl.when(pl.program_id(2) == 0)
    def _():
        vt_ref[:HEAD_W, :] = v_ref[0].astype(F32).T.astype(BF16)
        vt_ref[HEAD_W:, :] = jnp.ones((ONES_ROWS, vt_ref.shape[1]), BF16)

    dl = dl_ref[...]
    lam = (jnp.exp(jnp.sum(dl[0:1] * dl[1:2], axis=-1, keepdims=True))
           - jnp.exp(jnp.sum(dl[2:3] * dl[3:4], axis=-1, keepdims=True)) + lam_init)
    q = q_ref[0] * (DA ** -0.5)
    lane = lax.broadcasted_iota(jnp.int32, q.shape, 1)
    q_halves = (jnp.where(lane < DA, q, jnp.zeros_like(q)), jnp.where(lane >= DA, q, jnp.zeros_like(q)))
    n_blk = k_ref.shape[1] // KEY_BLK

    def score_block(half, j):
        rows = slice(j * KEY_BLK, (j + 1) * KEY_BLK)
        s = lax.dot_general(k_ref[0, rows, :], q_halves[half], NT_DIMS, preferred_element_type=F32)
        return s + b_ref[0, rows, :]

    m = [None, None]
    acc = [None, None]
    scores = [None, None]
    for j in range(n_blk + 1):
        pending = scores
        scores = [score_block(half, j) for half in range(2)] if j < n_blk else [None, None]
        if j == 0:
            continue
        rows = slice((j - 1) * KEY_BLK, j * KEY_BLK)
        for half in range(2):
            s = pending[half]
            m_blk = jnp.max(s, axis=0, keepdims=True)
            m_new = m_blk if m[half] is None else jnp.maximum(m[half], m_blk)
            pv = jnp.dot(vt_ref[:, rows], jnp.exp2(s - m_new).astype(BF16), preferred_element_type=F32)
            acc[half] = pv if acc[half] is None else acc[half] * jnp.exp2(m[half] - m_new) + pv
            m[half] = m_new

    o = (acc[0][:HEAD_W] / acc[0][HEAD_W:HEAD_W + 1]
         - lam * (acc[1][:HEAD_W] / acc[1][HEAD_W:HEAD_W + 1]))
    y = o * lax.rsqrt(jnp.mean(o * o, axis=0, keepdims=True) + EPS) * g_ref[...] * (1.0 - lam_init)
    o_ref[0] = y.T.astype(o_ref.dtype)


def _diff_attention(za, bias_t, dl, gain_col, lam_init, B, S, tq):
    return pl.pallas_call(
        functools.partial(_attn_kernel, lam_init=lam_init),
        grid=(N_HEADS, B, S // tq),
        in_specs=[
            pl.BlockSpec((4, DA), lambda h, b, q: (0, 0)),
            pl.BlockSpec((1, tq, HEAD_W), lambda h, b, q: (b, q, h)),
            pl.BlockSpec((1, S, HEAD_W), lambda h, b, q: (b, 0, N_HEADS + h)),
            pl.BlockSpec((1, S, HEAD_W), lambda h, b, q: (b, 0, 2 * N_HEADS + h)),
            pl.BlockSpec((1, S, tq), lambda h, b, q: (h, 0, q)),
            pl.BlockSpec((HEAD_W, 1), lambda h, b, q: (h, 0)),
        ],
        out_specs=pl.BlockSpec((1, tq, HEAD_W), lambda h, b, q: (b, q, h)),
        out_shape=jax.ShapeDtypeStruct((B, S, D_MODEL), BF16),
        scratch_shapes=[pltpu.VMEM((HEAD_W + ONES_ROWS, S), BF16)],
        compiler_params=_params(("parallel", "parallel", "arbitrary")),
        name="diff_attention",
    )(dl, za, za, za, bias_t, gain_col)


def _retention_kernel(dec_ref, q_ref, k_ref, v_ref, bg_ref, cos_ref, sin_ref, g_ref, o_ref,
                      qs_ref, kt_ref, acc_ref, kvf_ref, kvb_ref, r_ref, *, S):
    L = CHUNK
    n = S // L
    lane = lax.broadcasted_iota(jnp.int32, (S, LANES), 1)
    first_half = (lane & (DKB // 2)) == 0
    cos = cos_ref[...]
    sin = sin_ref[...]

    def rotary(t):
        partner = jnp.where(first_half, pltpu.roll(t, LANES - DKB // 2, 1), pltpu.roll(t, DKB // 2, 1))
        return t * cos + partner * sin

    qs_ref[...] = rotary(q_ref[0])
    ks = rotary(k_ref[0]) * (DKB ** -0.5)
    for c in range(n):
        kt_ref[c] = ks[c * L:(c + 1) * L].T

    row = lax.broadcasted_iota(jnp.int32, (L, LANES), 0).astype(F32)
    col = lax.broadcasted_iota(jnp.int32, (L, LANES), 1).astype(F32)
    lane_l = lax.broadcasted_iota(jnp.int32, (L, LANES), 1)
    pos_row = col[0:1]
    diff = row - col
    heads = []
    for j in range(2):
        lg = _log_sigmoid(dec_ref[j])
        lgf, lgb = lg[0:1], lg[1:2]
        heads.append(dict(
            mask=(lane_l // DKB) == j,
            dmat=(jnp.where(diff >= 0, jnp.exp(lgf * jnp.maximum(diff, 0.0)), 0.0)
                  + jnp.where(diff <= 0, jnp.exp(lgb * jnp.maximum(-diff, 0.0)), 0.0)),
            xi_f=jnp.exp(lgf * (row + 1.0)), xi_b=jnp.exp(lgb * (L - row)),
            zeta_f=jnp.exp(lgf * (L - 1.0 - pos_row)), zeta_b=jnp.exp(lgb * pos_row),
            decay_f=jnp.exp(lgf * L), decay_b=jnp.exp(lgb * L)))

    def prepare(c, carry):
        sl = pl.ds(pl.multiple_of(c * L, L), L)
        qc = qs_ref[sl, :]
        kt = kt_ref[c]
        ktb = kt.astype(BF16)
        for j, hd in enumerate(heads):
            vj = v_ref[0, sl, j * HEAD_W:(j + 1) * HEAD_W]
            qj = jnp.where(hd["mask"], qc, 0.0).astype(BF16)
            inner = jnp.dot(qj, ktb, preferred_element_type=F32) * hd["dmat"]
            acc_ref[j, sl, :] = jnp.dot(inner.astype(BF16), vj, preferred_element_type=F32)
            kvf_ref[j, c] = jnp.dot((kt * hd["zeta_f"]).astype(BF16), vj, preferred_element_type=F32)
            kvb_ref[j, c] = jnp.dot((kt * hd["zeta_b"]).astype(BF16), vj, preferred_element_type=F32)
        return carry

    lax.fori_loop(0, n, prepare, 0, unroll=2)
    r_ref[...] = jnp.zeros_like(r_ref)

    def recur(i, carry):
        for j, hd in enumerate(heads):
            for rev in (False, True):
                c = (n - 1 - i) if rev else i
                sl = pl.ds(pl.multiple_of(c * L, L), L)
                xi, decay, kv_ref = ((hd["xi_b"], hd["decay_b"], kvb_ref) if rev
                                     else (hd["xi_f"], hd["decay_f"], kvf_ref))
                r = r_ref[2 * j + rev]
                qj = (jnp.where(hd["mask"], qs_ref[sl, :], 0.0) * xi).astype(BF16)
                acc_ref[j, sl, :] += jnp.dot(qj, r.astype(BF16), preferred_element_type=F32)
                r_ref[2 * j + rev] = decay * r + kv_ref[j, c]
        return carry

    lax.fori_loop(0, n, recur, 0, unroll=2)
    for j in range(2):
        cols = slice(j * HEAD_W, (j + 1) * HEAD_W)
        y = _silu(bg_ref[0, :, cols]) * _layer_norm_head(acc_ref[j], g_ref[:, cols])
        o_ref[0, :, cols] = y.astype(o_ref.dtype)


def _retention(za, zb, dec, cos, sin, gain, B, S):
    qk_blk = 3 * N_HEADS
    pair_w = 2 * HEAD_W
    n = S // CHUNK
    return pl.pallas_call(
        functools.partial(_retention_kernel, S=S),
        grid=(B, N_HEADS // 2),
        in_specs=[
            pl.BlockSpec((2, 2, LANES), lambda b, p: (p, 0, 0)),
            pl.BlockSpec((1, S, LANES), lambda b, p: (b, 0, qk_blk + p)),
            pl.BlockSpec((1, S, LANES), lambda b, p: (b, 0, qk_blk + N_HEADS // 2 + p)),
            pl.BlockSpec((1, S, pair_w), lambda b, p: (b, 0, 3 * N_HEADS // 2 + p)),
            pl.BlockSpec((1, S, pair_w), lambda b, p: (b, 0, 4 * N_HEADS // 2 + p)),
            pl.BlockSpec((S, LANES), lambda b, p: (0, 0)),
            pl.BlockSpec((S, LANES), lambda b, p: (0, 0)),
            pl.BlockSpec((1, pair_w), lambda b, p: (0, p)),
        ],
        out_specs=pl.BlockSpec((1, S, pair_w), lambda b, p: (b, 0, p)),
        out_shape=jax.ShapeDtypeStruct((B, S, D_MODEL), BF16),
        scratch_shapes=[pltpu.VMEM((S, LANES), F32), pltpu.VMEM((n, LANES, CHUNK), F32),
                        pltpu.VMEM((2, S, HEAD_W), F32),
                        pltpu.VMEM((2, n, LANES, HEAD_W), F32), pltpu.VMEM((2, n, LANES, HEAD_W), F32),
                        pltpu.VMEM((4, LANES, HEAD_W), F32)],
        compiler_params=_params(("parallel", "arbitrary")),
        name="retention",
    )(dec, zb, zb, za, zb, cos, sin, gain)


def _mlstm_kernel(q_ref, k_ref, v_ref, gt_ref, gb_ref, cwq_ref, cwk_ref, co_ref, gn_ref, o_ref,
                  qs_ref, kt_ref, gsc_ref, va_ref, sv_ref, kv_ref, bcb_ref, dmx_ref, sc_ref, hf_ref, hb_ref,
                  c_ref, n_ref, *, S):
    L = CHUNK
    n = S // L
    h = pl.program_id(1)

    rows = lax.broadcasted_iota(jnp.int32, (S, HEAD_W), 0)

    def conv_silu(u, w):
        prev = jnp.where(rows == 0, 0.0, pltpu.roll(u, 1, 0))
        nxt = jnp.where(rows == S - 1, 0.0, pltpu.roll(u, S - 1, 0))
        return _silu(w[0:1] * prev + w[1:2] * u + w[2:3] * nxt)

    qs_ref[...] = conv_silu(q_ref[0], cwq_ref[...]).astype(BF16)
    va_ref[:, :HEAD_W] = v_ref[0]
    va_ref[:, HEAD_W:] = jnp.ones_like(v_ref[0])

    ks = conv_silu(k_ref[0], cwk_ref[...]) * (HEAD_W ** -0.5)
    gates = gt_ref[0] + gb_ref[...]
    gsub = lax.broadcasted_iota(jnp.int32, (GATE_COLS, L), 0)
    for c in range(n):
        kt_ref[c] = ks[c * L:(c + 1) * L].T
        gt = gates[c * L:(c + 1) * L].T[:GATE_COLS]
        pick = lambda j: jnp.sum(jnp.where(gsub == j, gt, 0.0), axis=0, keepdims=True)
        gsc_ref[c] = jnp.concatenate(
            [pick(h), _log_sigmoid(pick(N_HEADS + h)), pick(2 * N_HEADS + h),
             _log_sigmoid(pick(3 * N_HEADS + h)), jnp.zeros((4, L), F32)], axis=0)

    ri = lax.broadcasted_iota(jnp.int32, (L, L), 0)
    ci = lax.broadcasted_iota(jnp.int32, (L, L), 1)
    eye = ri == ci
    ones_rhs = jnp.ones((2 * L, LANES), BF16)

    causal = ((ci <= ri), (ci >= ri))
    grp = 2 if n % 2 == 0 else 1
    units = [(u, d) for u in range(grp) for d in range(2)]

    def prepare(i, carry):
        cs = [i * grp + u for u in range(grp)]
        sls = [pl.ds(pl.multiple_of(c * L, L), L) for c in cs]
        kts = [kt_ref[c] for c in cs]
        gs = [gsc_ref[c] for c in cs]
        vas = [va_ref[sl, :] for sl in sls]
        qks = [jnp.dot(qs_ref[sl, :], kt.astype(BF16), preferred_element_type=F32) for sl, kt in zip(sls, kts)]
        b_cbs = {}
        for u, d in units:
            lfm = jnp.where(causal[d], gs[u][2 * d + 1:2 * d + 2], 0.0)
            hi = lfm.astype(BF16)
            lo = (lfm - hi.astype(F32)).astype(BF16)
            b_cbs[u, d] = jnp.dot(jnp.concatenate([hi, lo], axis=1), ones_rhs, preferred_element_type=F32)
        s0s, ktws = {}, {}
        for u, d in units:
            b_cb = b_cbs[u, d]
            ii = gs[u][2 * d:2 * d + 1]
            b_row = jnp.sum(jnp.where(eye, b_cb, 0.0), axis=0, keepdims=True)
            b_last = b_cb[0:1] if d else b_cb[L - 1:L]
            r_row = ii - b_row
            dlog = jnp.where(causal[d], b_cb + r_row, NEG)
            dmax = jnp.max(dlog, axis=1, keepdims=True)
            s0s[u, d] = (qks[u] * jnp.exp(dlog - dmax)).astype(BF16)
            wlog = r_row + b_last[:, 0:1]
            wmax = jnp.max(wlog, axis=1, keepdims=True)
            ktws[u, d] = (kts[u] * jnp.exp(wlog - wmax)).astype(BF16)
            bcb_ref[d, cs[u]] = b_cb
            dmx_ref[d, cs[u]] = jnp.broadcast_to(dmax, (L, LANES))
            sc_ref[d, cs[u]] = jnp.concatenate(
                [b_last, jnp.broadcast_to(wmax, (1, LANES)), jnp.zeros((6, LANES), F32)], axis=0)
        for u, d in units:
            sv_ref[d, cs[u]] = jnp.dot(s0s[u, d], vas[u], preferred_element_type=F32)
        for u, d in units:
            kv_ref[d, cs[u]] = jnp.dot(ktws[u, d], vas[u], preferred_element_type=F32)
        return carry

    lax.fori_loop(0, n // grp, prepare, 0)

    c_ref[...] = jnp.zeros_like(c_ref)
    n_ref[...] = jnp.zeros_like(n_ref)

    def recur(i, ms):
        out = []
        for d in range(2):
            c = (n - 1 - i) if d else i
            m = ms[d]
            sl = pl.ds(pl.multiple_of(c * L, L), L)
            q = qs_ref[sl, :]
            b_cb, dmax, sv, kv, sc = bcb_ref[d, c], dmx_ref[d, c], sv_ref[d, c], kv_ref[d, c], sc_ref[d, c]
            b_last, wmax = sc[0:1], sc[1:2]
            inter = b_cb + m
            m_row = jnp.maximum(dmax, inter)
            a = jnp.exp(dmax - m_row)
            w_state = jnp.exp(inter - m_row)
            qc = jnp.dot(q, c_ref[d].astype(BF16), preferred_element_type=F32)
            qn = jnp.dot(q, n_ref[d].astype(BF16), preferred_element_type=F32)
            num = a * sv[:, :HEAD_W] + w_state * qc
            den = a * sv[:, HEAD_W:] + w_state * qn
            hh = num / jnp.maximum(jnp.abs(den), jnp.exp(-m_row))
            if d:
                hb_ref[sl, :] = hh
            else:
                hf_ref[sl, :] = hh
            m_new = jnp.maximum(b_last + m, wmax)
            beta = jnp.exp(wmax - m_new)
            decay = jnp.exp(b_last + m - m_new)
            c_ref[d] = decay * c_ref[d] + beta * kv[:, :HEAD_W]
            n_ref[d] = decay * n_ref[d] + beta * kv[:, HEAD_W:]
            out.append(m_new)
        return tuple(out)

    m0 = jnp.full((1, LANES), NEG, F32)
    lax.fori_loop(0, n, recur, (m0, m0), unroll=2)

    y = _sigmoid(co_ref[0]) * _layer_norm_head(hf_ref[...] + hb_ref[...], gn_ref[...])
    o_ref[0] = y.astype(o_ref.dtype)


def _mlstm(za, zb, zg, gate_b, cwq, cwk, gain, B, S):
    n = S // CHUNK
    per_chunk = lambda rows, cols: pltpu.VMEM((2, n, rows, cols), F32)
    return pl.pallas_call(
        functools.partial(_mlstm_kernel, S=S),
        grid=(B, N_HEADS),
        in_specs=[
            pl.BlockSpec((1, S, HEAD_W), lambda b, h: (b, 0, 5 * N_HEADS + h)),
            pl.BlockSpec((1, S, HEAD_W), lambda b, h: (b, 0, 6 * N_HEADS + h)),
            pl.BlockSpec((1, S, HEAD_W), lambda b, h: (b, 0, 4 * N_HEADS + h)),
            pl.BlockSpec((1, S, LANES), lambda b, h: (b, 0, 0)),
            pl.BlockSpec((1, LANES), lambda b, h: (0, 0)),
            pl.BlockSpec((CONV_W, HEAD_W), lambda b, h: (0, h)),
            pl.BlockSpec((CONV_W, HEAD_W), lambda b, h: (0, h)),
            pl.BlockSpec((1, S, HEAD_W), lambda b, h: (b, 0, 7 * N_HEADS + h)),
            pl.BlockSpec((1, HEAD_W), lambda b, h: (0, h)),
        ],
        out_specs=pl.BlockSpec((1, S, HEAD_W), lambda b, h: (b, 0, h)),
        out_shape=jax.ShapeDtypeStruct((B, S, D_MODEL), BF16),
        scratch_shapes=[
            pltpu.VMEM((S, HEAD_W), BF16),
            pltpu.VMEM((n, HEAD_W, CHUNK), F32),
            pltpu.VMEM((n, 8, CHUNK), F32),
            pltpu.VMEM((S, 2 * HEAD_W), BF16),
            per_chunk(CHUNK, 2 * HEAD_W),
            per_chunk(HEAD_W, 2 * HEAD_W),
            per_chunk(CHUNK, LANES),
            per_chunk(CHUNK, LANES),
            per_chunk(8, LANES),
            pltpu.VMEM((S, HEAD_W), F32), pltpu.VMEM((S, HEAD_W), F32),
            pltpu.VMEM((2, HEAD_W, HEAD_W), F32), pltpu.VMEM((2, HEAD_W, HEAD_W), F32),
        ],
        compiler_params=_params(("parallel", "arbitrary")),
        name="mlstm",
    )(zb, zb, za, zg, gate_b, cwq, cwk, zb, gain)


def _merge_kernel(x_ref, ya_ref, yb_ref, yc_ref, bgt_ref, gb_ref, wa_ref, wb_ref, wc_ref, wo_ref,
                  g2_ref, wr_ref, xo_ref, xn_ref, lg_ref):
    D = D_MODEL
    gate = lambda j: _sigmoid(bgt_ref[:, j * D:(j + 1) * D] + gb_ref[j:j + 1, :])
    mixed = (gate(0) * jnp.dot(ya_ref[...], wa_ref[...], preferred_element_type=F32)
             + gate(1) * jnp.dot(yb_ref[...], wb_ref[...], preferred_element_type=F32)
             + gate(2) * jnp.dot(yc_ref[...], wc_ref[...], preferred_element_type=F32))
    x = x_ref[...] + jnp.dot(mixed.astype(BF16), wo_ref[...], preferred_element_type=F32)
    xo_ref[...] = x
    xn = _rms(x, g2_ref[...]).astype(BF16)
    xn_ref[...] = xn
    lg_ref[...] = jnp.dot(xn, wr_ref[...], preferred_element_type=F32)


def _merge(x, ya, yb, yc, zb, gate_b, wa, wb, wc, wo, g2, wr, tm):
    T, D = x.shape
    row = lambda i: (i, 0)
    const = lambda i: (0, 0)
    return pl.pallas_call(
        _merge_kernel,
        grid=(T // tm,),
        in_specs=[
            pl.BlockSpec((tm, D), row), pl.BlockSpec((tm, D), row), pl.BlockSpec((tm, D), row),
            pl.BlockSpec((tm, D), row), pl.BlockSpec((tm, 3 * D), row), pl.BlockSpec((3, D), const),
            pl.BlockSpec((D, D), const), pl.BlockSpec((D, D), const), pl.BlockSpec((D, D), const),
            pl.BlockSpec((D, D), const), pl.BlockSpec((1, D), const), pl.BlockSpec((D, LANES), const),
        ],
        out_specs=[pl.BlockSpec((tm, D), row), pl.BlockSpec((tm, D), row), pl.BlockSpec((tm, LANES), row)],
        out_shape=[jax.ShapeDtypeStruct((T, D), F32), jax.ShapeDtypeStruct((T, D), BF16),
                   jax.ShapeDtypeStruct((T, LANES), F32)],
        compiler_params=_params(("parallel",)),
        name="merge",
    )(x, ya, yb, yc, zb, gate_b, wa, wb, wc, wo, g2, wr)


def _router_kernel(lg_ref, tri_ref, aff_ref, pos_ref, *, cap):
    E = N_EXPERTS
    logits = lg_ref[0].T[:E]
    e = jnp.exp(logits - jnp.max(logits, axis=0, keepdims=True))
    aff = e / jnp.sum(e, axis=0, keepdims=True)
    key = pltpu.bitcast(aff, jnp.int32)

    def refine(i, thr):
        cand = thr | (jnp.int32(1) << (30 - i))
        cnt = jnp.sum(jnp.where(key >= cand, 1.0, 0.0), axis=1, keepdims=True)
        return jnp.where(cnt >= cap, cand, thr)

    thr = lax.fori_loop(0, 31, refine, jnp.zeros((E, 1), jnp.int32))
    above = key > thr
    tied = key == thr
    need = cap - jnp.sum(jnp.where(above, 1.0, 0.0), axis=1, keepdims=True)
    tri = tri_ref[...]
    tie_rank = jnp.dot(jnp.where(tied, 1.0, 0.0).astype(BF16), tri, preferred_element_type=F32)
    chosen = jnp.where(above, 1.0, jnp.where(tied, jnp.where(tie_rank < need, 1.0, 0.0), 0.0))
    slot = jnp.dot(chosen.astype(BF16), tri, preferred_element_type=F32)
    aff_ref[0] = aff
    pos_ref[0] = jnp.where(chosen > 0.0, slot, -1.0)


def _router(logits, tri, B, S, cap):
    out = jax.ShapeDtypeStruct((B, N_EXPERTS, S), F32)
    blk = pl.BlockSpec((1, N_EXPERTS, S), lambda b: (b, 0, 0))
    return pl.pallas_call(
        functools.partial(_router_kernel, cap=cap),
        grid=(B,),
        in_specs=[pl.BlockSpec((1, S, LANES), lambda b: (b, 0, 0)), pl.BlockSpec((S, S), lambda b: (0, 0))],
        out_specs=[blk, blk],
        out_shape=[out, out],
        compiler_params=_params(("parallel",)),
        name="router",
    )(logits, tri)


def _moe_kernel(xn_ref, pos_ref, aff_ref, wg_ref, wu_ref, wd_ref, o_ref,
                sel_ref, xe_ref, y_ref, gate_ref, *, S, cap, row_blk):
    e, f = pl.program_id(1), pl.program_id(2)

    @pl.when((e == 0) & (f == 0))
    def _():
        o_ref[...] = jnp.zeros_like(o_ref)

    @pl.when(f == 0)
    def _():
        slot = lax.broadcasted_iota(jnp.int32, (cap, S), 0).astype(F32)
        hit = slot == pos_ref[0, 0]
        sel = jnp.where(hit, 1.0, 0.0).astype(BF16)
        sel_ref[...] = sel
        xe_ref[...] = jnp.dot(sel, xn_ref[0], preferred_element_type=F32).astype(BF16)
        gate = jnp.sum(jnp.where(hit, aff_ref[0, 0], 0.0), axis=1, keepdims=True)
        gate_ref[...] = jnp.broadcast_to(gate, gate_ref.shape)
        y_ref[...] = jnp.zeros_like(y_ref)

    xe = xe_ref[...]
    hid = (_silu(jnp.dot(xe, wg_ref[0], preferred_element_type=F32))
           * jnp.dot(xe, wu_ref[0], preferred_element_type=F32))
    y_ref[...] += jnp.dot(hid.astype(BF16), wd_ref[0], preferred_element_type=F32)

    @pl.when(f == pl.num_programs(2) - 1)
    def _():
        y = (y_ref[...] * gate_ref[:, 0:1]).astype(BF16)
        for r in range(S // row_blk):
            rows = slice(r * row_blk, (r + 1) * row_blk)
            o_ref[0, rows, :] += lax.dot_general(sel_ref[:, rows], y, TN_DIMS, preferred_element_type=F32)


def _moe(xn, pos, aff, wg, wu, wd, B, S, cap, tf):
    D = D_MODEL
    row_blk = min(S, 512)
    return pl.pallas_call(
        functools.partial(_moe_kernel, S=S, cap=cap, row_blk=row_blk),
        grid=(B, N_EXPERTS, D_FF // tf),
        in_specs=[
            pl.BlockSpec((1, S, D), lambda b, e, f: (b, 0, 0)),
            pl.BlockSpec((1, 1, 1, S), lambda b, e, f: (b, e, 0, 0)),
            pl.BlockSpec((1, 1, 1, S), lambda b, e, f: (b, e, 0, 0)),
            pl.BlockSpec((1, D, tf), lambda b, e, f: (e, 0, f)),
            pl.BlockSpec((1, D, tf), lambda b, e, f: (e, 0, f)),
            pl.BlockSpec((1, tf, D), lambda b, e, f: (e, f, 0)),
        ],
        out_specs=pl.BlockSpec((1, S, D), lambda b, e, f: (b, 0, 0)),
        out_shape=jax.ShapeDtypeStruct((B, S, D), F32),
        scratch_shapes=[pltpu.VMEM((cap, S), BF16), pltpu.VMEM((cap, D), BF16),
                        pltpu.VMEM((cap, D), F32), pltpu.VMEM((cap, LANES), F32)],
        compiler_params=_params(("parallel", "arbitrary", "arbitrary")),
        name="moe",
    )(xn, pos, aff, wg, wu, wd)


def _final_norm_kernel(x_ref, r_ref, g_ref, o_ref):
    o_ref[...] = _rms(x_ref[...] + r_ref[...], g_ref[...])


def _final_norm(x, r, g, tm):
    T, D = x.shape
    row = pl.BlockSpec((tm, D), lambda i: (i, 0))
    return pl.pallas_call(
        _final_norm_kernel,
        grid=(T // tm,),
        in_specs=[row, row, pl.BlockSpec((1, D), lambda i: (0, 0))],
        out_specs=row,
        out_shape=jax.ShapeDtypeStruct((T, D), F32),
        compiler_params=_params(("parallel",)),
        name="final_norm",
    )(x, r, g)


def _rotary_tables(S):
    half = DKB // 2
    freqs = ROPE_BASE ** (-jnp.arange(half, dtype=F32) / half)
    ang = jnp.arange(S, dtype=F32)[:, None] * freqs[None, :]
    cos, sin = jnp.cos(ang), jnp.sin(ang)
    reps = LANES // DKB
    return jnp.tile(cos, (1, 2 * reps)), jnp.tile(jnp.concatenate([-sin, sin], axis=1), (1, reps))


def _pad_cols(a, width):
    return jnp.pad(a, ((0, 0), (0, width - a.shape[1])))


def kernel(x, t5_bias, attn_norm_g, w_in, branch_gate_b, diff_lambda, diff_norm_g, ret_decay_logit, ret_norm_g, mlstm_conv_w, mlstm_gate_b, mlstm_norm_g, w_branch_a, w_branch_b, w_branch_c, w_out, ffn_norm_g, w_router, w_exp_gate, w_exp_up, w_exp_down, final_norm_g):
    B, S, D = x.shape
    assert D == D_MODEL and S % CHUNK == 0
    T = B * S
    depth = w_in.shape[0]
    cap = CAPACITY_FACTOR * S // N_EXPERTS
    tm = min(1024, T)
    tq = min(512, S)
    tf = D_FF // 2

    bias = _t5_bias(t5_bias.astype(F32), S, min(256, S))
    cos, sin = _rotary_tables(S)
    tri = (jnp.arange(S)[:, None] < jnp.arange(S)[None, :]).astype(BF16)

    x = x.reshape(T, D).astype(F32)
    ffn = None
    for l in range(depth):
        lam_init = 0.8 - 0.6 * math.exp(-0.3 * l)
        (w_aq, w_ak, w_av, w_bq, w_bk, w_bv, w_bg, w_cq, w_ck, w_cv, w_co, w_cgt, w_bgt) = jnp.split(
            w_in[l], _IN_OFFSETS, axis=-1)
        wa_cols = jnp.concatenate([w_aq * LOG2E, w_ak, w_av, w_bv, w_cv], axis=1).astype(BF16)
        wb_cols = jnp.concatenate([w_bgt, w_bq, w_bk, w_bg, w_cq, w_ck, w_co], axis=1).astype(BF16)
        wg_cols = _pad_cols(w_cgt, LANES).astype(BF16)
        g1 = attn_norm_g[l].reshape(1, D)

        if ffn is None:
            za = _norm_matmul(x, g1, wa_cols, BF16, tm, D)
        else:
            x, za = _add_norm_matmul(x, ffn, g1, wa_cols, BF16, tm, D)
        za = za.reshape(B, S, -1)
        zb = _norm_matmul(x, g1, wb_cols, F32, tm, D)
        zg = _norm_matmul(x, g1, wg_cols, F32, tm, LANES).reshape(B, S, LANES)
        zb3 = zb.reshape(B, S, -1)

        ya = _diff_attention(za, bias, diff_lambda[l].astype(F32), diff_norm_g[l].reshape(D, 1), lam_init, B, S, tq)
        dec = jnp.broadcast_to(ret_decay_logit[l].astype(F32).T[:, :, None], (N_HEADS, 2, LANES))
        yb = _retention(za, zb3, dec, cos, sin, ret_norm_g[l].reshape(1, D), B, S)
        gate_b = _pad_cols(mlstm_gate_b[l].astype(F32).reshape(1, GATE_COLS), LANES)
        cw = mlstm_conv_w[l].astype(F32)
        yc = _mlstm(za, zb3, zg, gate_b, cw[:, :D], cw[:, D:], mlstm_norm_g[l].reshape(1, D), B, S)

        x, xn, logits = _merge(
            x, ya.reshape(T, D), yb.reshape(T, D), yc.reshape(T, D), zb, branch_gate_b[l].astype(F32),
            w_branch_a[l].astype(BF16), w_branch_b[l].astype(BF16), w_branch_c[l].astype(BF16),
            w_out[l].astype(BF16), ffn_norm_g[l].reshape(1, D), _pad_cols(w_router[l], LANES).astype(BF16),
            min(256, T))

        aff, pos = _router(logits.reshape(B, S, LANES), tri, B, S, cap)
        ffn = _moe(xn.reshape(B, S, D), pos.reshape(B, N_EXPERTS, 1, S),
                   aff.reshape(B, N_EXPERTS, 1, S), w_exp_gate[l].astype(BF16), w_exp_up[l].astype(BF16),
                   w_exp_down[l].astype(BF16), B, S, cap, tf).reshape(T, D)
    return _final_norm(x, ffn, final_norm_g.reshape(1, D), tm).reshape(B, S, D)


_IN_WIDTHS = (N_HEADS * 2 * DA, N_HEADS * 2 * DA, N_HEADS * HEAD_W, N_HEADS * DKB, N_HEADS * DKB,
              N_HEADS * HEAD_W, N_HEADS * HEAD_W, N_HEADS * HEAD_W, N_HEADS * HEAD_W, N_HEADS * HEAD_W,
              N_HEADS * HEAD_W, GATE_COLS, 3 * D_MODEL)
_IN_OFFSETS = tuple(int(sum(_IN_WIDTHS[:i + 1])) for i in range(len(_IN_WIDTHS) - 1))
```

```python
import functools
import math

import jax
import jax.numpy as jnp
from jax import lax
from jax.experimental import pallas as pl
from jax.experimental.pallas import tpu as pltpu

F32 = jnp.float32
BF16 = jnp.bfloat16

D_MODEL = 1024
N_HEADS = 8
HEAD_W = 128
DA = 64
DKB = 64
CHUNK = 128
CONV_W = 3
N_EXPERTS = 16
D_FF = 2048
CAPACITY_FACTOR = 2
T5_BUCKETS = 32
T5_MAX_DIST = 128
ROPE_BASE = 10000.0
EPS = 1e-6
NEG = -1e30
GATE_COLS = 32
LOG2E = math.log2(math.e)
LANES = 128

VMEM_LIMIT = 56 * 1024 * 1024

NT_DIMS = (((1,), (1,)), ((), ()))
TN_DIMS = (((0,), (0,)), ((), ()))


def _params(sem):
    return pltpu.CompilerParams(dimension_semantics=sem, vmem_limit_bytes=VMEM_LIMIT)


def _log_sigmoid(x):
    return -(jnp.maximum(-x, 0.0) + jnp.log1p(jnp.exp(-jnp.abs(x))))


def _sigmoid(x):
    return 1.0 / (1.0 + jnp.exp(-x))


def _silu(x):
    return x * _sigmoid(x)


def _rms(x, g):
    return x * lax.rsqrt(jnp.mean(x * x, axis=-1, keepdims=True) + EPS) * g


def _layer_norm_head(h, gain):
    h = h - jnp.mean(h, axis=-1, keepdims=True)
    return h * lax.rsqrt(jnp.mean(h * h, axis=-1, keepdims=True) + EPS) * gain


def _norm_matmul_kernel(x_ref, g_ref, w_ref, o_ref, xn_ref):
    @pl.when(pl.program_id(1) == 0)
    def _():
        xn_ref[...] = _rms(x_ref[...], g_ref[...]).astype(BF16)

    o_ref[...] = jnp.dot(xn_ref[...], w_ref[...], preferred_element_type=F32).astype(o_ref.dtype)


def _norm_matmul(x, g, w, out_dtype, tm, tn):
    T, D = x.shape
    N = w.shape[1]
    return pl.pallas_call(
        _norm_matmul_kernel,
        grid=(T // tm, N // tn),
        in_specs=[
            pl.BlockSpec((tm, D), lambda i, j: (i, 0)),
            pl.BlockSpec((1, D), lambda i, j: (0, 0)),
            pl.BlockSpec((D, tn), lambda i, j: (0, j)),
        ],
        out_specs=pl.BlockSpec((tm, tn), lambda i, j: (i, j)),
        out_shape=jax.ShapeDtypeStruct((T, N), out_dtype),
        scratch_shapes=[pltpu.VMEM((tm, D), BF16)],
        compiler_params=_params(("parallel", "arbitrary")),
        name="norm_matmul",
    )(x, g, w)


def _add_norm_matmul_kernel(x_ref, r_ref, g_ref, w_ref, xs_ref, o_ref, xn_ref):
    @pl.when(pl.program_id(1) == 0)
    def _():
        xs = x_ref[...] + r_ref[...]
        xs_ref[...] = xs
        xn_ref[...] = _rms(xs, g_ref[...]).astype(BF16)

    o_ref[...] = jnp.dot(xn_ref[...], w_ref[...], preferred_element_type=F32).astype(o_ref.dtype)


def _add_norm_matmul(x, r, g, w, out_dtype, tm, tn):
    T, D = x.shape
    N = w.shape[1]
    row = pl.BlockSpec((tm, D), lambda i, j: (i, 0))
    return pl.pallas_call(
        _add_norm_matmul_kernel,
        grid=(T // tm, N // tn),
        in_specs=[row, row, pl.BlockSpec((1, D), lambda i, j: (0, 0)), pl.BlockSpec((D, tn), lambda i, j: (0, j))],
        out_specs=[row, pl.BlockSpec((tm, tn), lambda i, j: (i, j))],
        out_shape=[jax.ShapeDtypeStruct((T, D), F32), jax.ShapeDtypeStruct((T, N), out_dtype)],
        scratch_shapes=[pltpu.VMEM((tm, D), BF16)],
        compiler_params=_params(("parallel", "arbitrary")),
        name="add_norm_matmul",
    )(x, r, g, w)


def _t5_bias_kernel(tab_ref, bkt_ref, o_ref, *, tk, S):
    h = pl.program_id(0)
    ki = pl.program_id(1)
    bkt = bkt_ref[...]
    r = jnp.zeros((1, 2 * S), F32)
    for b in range(T5_BUCKETS):
        r = jnp.where(bkt == b, tab_ref[b, h] * LOG2E, r)
    x = jnp.broadcast_to(r, (tk, 2 * S))
    rolled = pltpu.roll(x, ki * tk + S, 1, stride=1, stride_axis=0)
    o_ref[0] = rolled[:, :S]


def _t5_bucket(rel):
    half = T5_BUCKETS // 2
    max_exact = half // 2
    offset = jnp.where(rel > 0, half, 0)
    n = jnp.abs(rel)
    large = max_exact + (jnp.log(jnp.maximum(n, 1).astype(F32) / max_exact)
                         / math.log(T5_MAX_DIST / max_exact) * (half - max_exact)).astype(jnp.int32)
    large = jnp.minimum(large, half - 1)
    return offset + jnp.where(n < max_exact, n, large)


def _t5_bias(table, S, tk):
    rel = S - jnp.arange(2 * S, dtype=jnp.int32)
    bkt = _t5_bucket(rel).astype(jnp.int32).reshape(1, 2 * S)
    return pl.pallas_call(
        functools.partial(_t5_bias_kernel, tk=tk, S=S),
        grid=(N_HEADS, S // tk),
        in_specs=[
            pl.BlockSpec(memory_space=pltpu.SMEM),
            pl.BlockSpec((1, 2 * S), lambda h, k: (0, 0)),
        ],
        out_specs=pl.BlockSpec((1, tk, S), lambda h, k: (h, k, 0)),
        out_shape=jax.ShapeDtypeStruct((N_HEADS, S, S), F32),
        compiler_params=_params(("parallel", "arbitrary")),
        name="t5_bias",
    )(table, bkt)


ONES_ROWS = 16


KEY_BLK = 512
BF16_ROWS = 16


def _attn_kernel(dl_ref, q_ref, k_ref, v_ref, b_ref, g_ref, o_ref, vt_ref, *, lam_init):
    @pl.when(pl.program_id(2) == 0)
    def _():
        vt_ref[:HEAD_W, :] = v_ref[0].astype(F32).T.astype(BF16)
        vt_ref[HEAD_W:, :] = jnp.ones((ONES_ROWS, vt_ref.shape[1]), BF16)

    dl = dl_ref[...]
    lam = (jnp.exp(jnp.sum(dl[0:1] * dl[1:2], axis=-1, keepdims=True))
           - jnp.exp(jnp.sum(dl[2:3] * dl[3:4], axis=-1, keepdims=True)) + lam_init)
    q = q_ref[0] * (DA ** -0.5)
    lane = lax.broadcasted_iota(jnp.int32, q.shape, 1)
    q_halves = (jnp.where(lane < DA, q, jnp.zeros_like(q)), jnp.where(lane >= DA, q, jnp.zeros_like(q)))
    key_blk = min(KEY_BLK, k_ref.shape[1])
    n_blk = k_ref.shape[1] // key_blk

    def score_block(half, j):
        rows = slice(j * key_blk, (j + 1) * key_blk)
        s = lax.dot_general(k_ref[0, rows, :], q_halves[half], NT_DIMS, preferred_element_type=F32)
        return (s + b_ref[0, rows, :]).astype(BF16)

    m = [None, None]
    acc = [None, None]
    scores = [None, None]
    for j in range(n_blk + 1):
        pending = scores
        scores = [score_block(half, j) for half in range(2)] if j < n_blk else [None, None]
        if j == 0:
            continue
        rows = slice((j - 1) * key_blk, j * key_blk)
        for half in range(2):
            s = pending[half]
            tile_max = jnp.max(s.reshape(key_blk // BF16_ROWS, BF16_ROWS, s.shape[1]), axis=0)
            m_blk = jnp.max(tile_max.astype(F32), axis=0, keepdims=True)
            m_new = m_blk if m[half] is None else jnp.maximum(m[half], m_blk)
            e = jnp.exp2(s - m_new.astype(BF16))
            pv = jnp.dot(vt_ref[:, rows], e, preferred_element_type=F32)
            acc[half] = pv if acc[half] is None else acc[half] * jnp.exp2(m[half] - m_new) + pv
            m[half] = m_new

    o = (acc[0][:HEAD_W] / acc[0][HEAD_W:HEAD_W + 1]
         - lam * (acc[1][:HEAD_W] / acc[1][HEAD_W:HEAD_W + 1]))
    y = o * lax.rsqrt(jnp.mean(o * o, axis=0, keepdims=True) + EPS) * g_ref[...] * (1.0 - lam_init)
    o_ref[0] = y.T.astype(o_ref.dtype)


def _diff_attention(za, bias_t, dl, gain_col, lam_init, B, S, tq):
    return pl.pallas_call(
        functools.partial(_attn_kernel, lam_init=lam_init),
        grid=(N_HEADS, B, S // tq),
        in_specs=[
            pl.BlockSpec((4, DA), lambda h, b, q: (0, 0)),
            pl.BlockSpec((1, tq, HEAD_W), lambda h, b, q: (b, q, h)),
            pl.BlockSpec((1, S, HEAD_W), lambda h, b, q: (b, 0, N_HEADS + h)),
            pl.BlockSpec((1, S, HEAD_W), lambda h, b, q: (b, 0, 2 * N_HEADS + h)),
            pl.BlockSpec((1, S, tq), lambda h, b, q: (h, 0, q)),
            pl.BlockSpec((HEAD_W, 1), lambda h, b, q: (h, 0)),
        ],
        out_specs=pl.BlockSpec((1, tq, HEAD_W), lambda h, b, q: (b, q, h)),
        out_shape=jax.ShapeDtypeStruct((B, S, D_MODEL), BF16),
        scratch_shapes=[pltpu.VMEM((HEAD_W + ONES_ROWS, S), BF16)],
        compiler_params=_params(("parallel", "parallel", "arbitrary")),
        name="diff_attention",
    )(dl, za, za, za, bias_t, gain_col)


def _retention_kernel(dec_ref, q_ref, k_ref, v_ref, bg_ref, cos_ref, sin_ref, g_ref, o_ref,
                      qs_ref, kt_ref, acc_ref, kvf_ref, kvb_ref, r_ref, *, S):
    L = CHUNK
    n = S // L
    lane = lax.broadcasted_iota(jnp.int32, (S, LANES), 1)
    first_half = (lane & (DKB // 2)) == 0
    cos = cos_ref[...]
    sin = sin_ref[...]

    def rotary(t):
        partner = jnp.where(first_half, pltpu.roll(t, LANES - DKB // 2, 1), pltpu.roll(t, DKB // 2, 1))
        return t * cos + partner * sin

    qs_ref[...] = rotary(q_ref[0])
    ks = rotary(k_ref[0]) * (DKB ** -0.5)
    for c in range(n):
        kt_ref[c] = ks[c * L:(c + 1) * L].T

    row = lax.broadcasted_iota(jnp.int32, (L, LANES), 0).astype(F32)
    col = lax.broadcasted_iota(jnp.int32, (L, LANES), 1).astype(F32)
    lane_l = lax.broadcasted_iota(jnp.int32, (L, LANES), 1)
    pos_row = col[0:1]
    diff = row - col
    heads = []
    for j in range(2):
        lg = _log_sigmoid(dec_ref[j])
        lgf, lgb = lg[0:1], lg[1:2]
        heads.append(dict(
            mask=(lane_l // DKB) == j,
            dmat=(jnp.where(diff >= 0, jnp.exp(lgf * jnp.maximum(diff, 0.0)), 0.0)
                  + jnp.where(diff <= 0, jnp.exp(lgb * jnp.maximum(-diff, 0.0)), 0.0)),
            xi_f=jnp.exp(lgf * (row + 1.0)), xi_b=jnp.exp(lgb * (L - row)),
            zeta_f=jnp.exp(lgf * (L - 1.0 - pos_row)), zeta_b=jnp.exp(lgb * pos_row),
            decay_f=jnp.exp(lgf * L), decay_b=jnp.exp(lgb * L)))

    def prepare(c, carry):
        sl = pl.ds(pl.multiple_of(c * L, L), L)
        qc = qs_ref[sl, :]
        kt = kt_ref[c]
        ktb = kt.astype(BF16)
        for j, hd in enumerate(heads):
            vj = v_ref[0, sl, j * HEAD_W:(j + 1) * HEAD_W]
            qj = jnp.where(hd["mask"], qc, 0.0).astype(BF16)
            inner = jnp.dot(qj, ktb, preferred_element_type=F32) * hd["dmat"]
            acc_ref[j, sl, :] = jnp.dot(inner.astype(BF16), vj, preferred_element_type=F32)
            kvf_ref[j, c] = jnp.dot((kt * hd["zeta_f"]).astype(BF16), vj, preferred_element_type=F32)
            kvb_ref[j, c] = jnp.dot((kt * hd["zeta_b"]).astype(BF16), vj, preferred_element_type=F32)
        return carry

    lax.fori_loop(0, n, prepare, 0, unroll=2)
    r_ref[...] = jnp.zeros_like(r_ref)

    def recur(i, carry):
        for j, hd in enumerate(heads):
            for rev in (False, True):
                c = (n - 1 - i) if rev else i
                sl = pl.ds(pl.multiple_of(c * L, L), L)
                xi, decay, kv_ref = ((hd["xi_b"], hd["decay_b"], kvb_ref) if rev
                                     else (hd["xi_f"], hd["decay_f"], kvf_ref))
                r = r_ref[2 * j + rev]
                qj = (jnp.where(hd["mask"], qs_ref[sl, :], 0.0) * xi).astype(BF16)
                acc_ref[j, sl, :] += jnp.dot(qj, r.astype(BF16), preferred_element_type=F32)
                r_ref[2 * j + rev] = decay * r + kv_ref[j, c]
        return carry

    lax.fori_loop(0, n, recur, 0, unroll=2)
    for j in range(2):
        cols = slice(j * HEAD_W, (j + 1) * HEAD_W)
        y = _silu(bg_ref[0, :, cols]) * _layer_norm_head(acc_ref[j], g_ref[:, cols])
        o_ref[0, :, cols] = y.astype(o_ref.dtype)


def _retention(za, zb, dec, cos, sin, gain, B, S):
    qk_blk = 3 * N_HEADS
    pair_w = 2 * HEAD_W
    n = S // CHUNK
    return pl.pallas_call(
        functools.partial(_retention_kernel, S=S),
        grid=(B, N_HEADS // 2),
        in_specs=[
            pl.BlockSpec((2, 2, LANES), lambda b, p: (p, 0, 0)),
            pl.BlockSpec((1, S, LANES), lambda b, p: (b, 0, qk_blk + p)),
            pl.BlockSpec((1, S, LANES), lambda b, p: (b, 0, qk_blk + N_HEADS // 2 + p)),
            pl.BlockSpec((1, S, pair_w), lambda b, p: (b, 0, 3 * N_HEADS // 2 + p)),
            pl.BlockSpec((1, S, pair_w), lambda b, p: (b, 0, 4 * N_HEADS // 2 + p)),
            pl.BlockSpec((S, LANES), lambda b, p: (0, 0)),
            pl.BlockSpec((S, LANES), lambda b, p: (0, 0)),
            pl.BlockSpec((1, pair_w), lambda b, p: (0, p)),
        ],
        out_specs=pl.BlockSpec((1, S, pair_w), lambda b, p: (b, 0, p)),
        out_shape=jax.ShapeDtypeStruct((B, S, D_MODEL), BF16),
        scratch_shapes=[pltpu.VMEM((S, LANES), F32), pltpu.VMEM((n, LANES, CHUNK), F32),
                        pltpu.VMEM((2, S, HEAD_W), F32),
                        pltpu.VMEM((2, n, LANES, HEAD_W), F32), pltpu.VMEM((2, n, LANES, HEAD_W), F32),
                        pltpu.VMEM((4, LANES, HEAD_W), F32)],
        compiler_params=_params(("parallel", "arbitrary")),
        name="retention",
    )(dec, zb, zb, za, zb, cos, sin, gain)


def _mlstm_kernel(q_ref, k_ref, v_ref, gt_ref, gb_ref, cwq_ref, cwk_ref, co_ref, gn_ref, o_ref,
                  qs_ref, kt_ref, gsc_ref, va_ref, sv_ref, kv_ref, bcb_ref, dmx_ref, sc_ref, hf_ref, hb_ref,
                  c_ref, n_ref, *, S):
    L = CHUNK
    n = S // L
    h = pl.program_id(1)

    rows = lax.broadcasted_iota(jnp.int32, (S, HEAD_W), 0)

    def conv_silu(u, w):
        prev = jnp.where(rows == 0, 0.0, pltpu.roll(u, 1, 0))
        nxt = jnp.where(rows == S - 1, 0.0, pltpu.roll(u, S - 1, 0))
        return _silu(w[0:1] * prev + w[1:2] * u + w[2:3] * nxt)

    qs_ref[...] = conv_silu(q_ref[0], cwq_ref[...]).astype(BF16)
    va_ref[:, :HEAD_W] = v_ref[0]
    va_ref[:, HEAD_W:] = jnp.ones_like(v_ref[0])

    ks = conv_silu(k_ref[0], cwk_ref[...]) * (HEAD_W ** -0.5)
    gates = gt_ref[0] + gb_ref[...]
    gsub = lax.broadcasted_iota(jnp.int32, (GATE_COLS, L), 0)
    for c in range(n):
        kt_ref[c] = ks[c * L:(c + 1) * L].T
        gt = gates[c * L:(c + 1) * L].T[:GATE_COLS]
        pick = lambda j: jnp.sum(jnp.where(gsub == j, gt, 0.0), axis=0, keepdims=True)
        gsc_ref[c] = jnp.concatenate(
            [pick(h), _log_sigmoid(pick(N_HEADS + h)), pick(2 * N_HEADS + h),
             _log_sigmoid(pick(3 * N_HEADS + h)), jnp.zeros((4, L), F32)], axis=0)

    ri = lax.broadcasted_iota(jnp.int32, (L, L), 0)
    ci = lax.broadcasted_iota(jnp.int32, (L, L), 1)
    eye = ri == ci
    ones_rhs = jnp.ones((2 * L, LANES), BF16)

    causal = ((ci <= ri), (ci >= ri))
    grp = 2 if n % 2 == 0 else 1
    units = [(u, d) for u in range(grp) for d in range(2)]

    def prepare(i, carry):
        cs = [i * grp + u for u in range(grp)]
        sls = [pl.ds(pl.multiple_of(c * L, L), L) for c in cs]
        kts = [kt_ref[c] for c in cs]
        gs = [gsc_ref[c] for c in cs]
        vas = [va_ref[sl, :] for sl in sls]
        qks = [jnp.dot(qs_ref[sl, :], kt.astype(BF16), preferred_element_type=F32) for sl, kt in zip(sls, kts)]
        b_cbs = {}
        for u, d in units:
            lfm = jnp.where(causal[d], gs[u][2 * d + 1:2 * d + 2], 0.0)
            hi = lfm.astype(BF16)
            lo = (lfm - hi.astype(F32)).astype(BF16)
            b_cbs[u, d] = jnp.dot(jnp.concatenate([hi, lo], axis=1), ones_rhs, preferred_element_type=F32)
        s0s, ktws = {}, {}
        for u, d in units:
            b_cb = b_cbs[u, d]
            ii = gs[u][2 * d:2 * d + 1]
            b_row = jnp.sum(jnp.where(eye, b_cb, 0.0), axis=0, keepdims=True)
            b_last = b_cb[0:1] if d else b_cb[L - 1:L]
            r_row = ii - b_row
            dlog = jnp.where(causal[d], b_cb + r_row, NEG)
            dmax = jnp.max(dlog, axis=1, keepdims=True)
            s0s[u, d] = (qks[u] * jnp.exp(dlog - dmax)).astype(BF16)
            wlog = r_row + b_last[:, 0:1]
            wmax = jnp.max(wlog, axis=1, keepdims=True)
            ktws[u, d] = (kts[u] * jnp.exp(wlog - wmax)).astype(BF16)
            bcb_ref[d, cs[u]] = b_cb
            dmx_ref[d, cs[u]] = jnp.broadcast_to(dmax, (L, LANES))
            sc_ref[d, cs[u]] = jnp.concatenate(
                [b_last, jnp.broadcast_to(wmax, (1, LANES)), jnp.zeros((6, LANES), F32)], axis=0)
        for u, d in units:
            sv_ref[d, cs[u]] = jnp.dot(s0s[u, d], vas[u], preferred_element_type=F32)
        for u, d in units:
            kv_ref[d, cs[u]] = jnp.dot(ktws[u, d], vas[u], preferred_element_type=F32)
        return carry

    lax.fori_loop(0, n // grp, prepare, 0)

    c_ref[...] = jnp.zeros_like(c_ref)
    n_ref[...] = jnp.zeros_like(n_ref)

    def recur(i, ms):
        out = []
        for d in range(2):
            c = (n - 1 - i) if d else i
            m = ms[d]
            sl = pl.ds(pl.multiple_of(c * L, L), L)
            q = qs_ref[sl, :]
            b_cb, dmax, sv, kv, sc = bcb_ref[d, c], dmx_ref[d, c], sv_ref[d, c], kv_ref[d, c], sc_ref[d, c]
            b_last, wmax = sc[0:1], sc[1:2]
            inter = b_cb + m
            m_row = jnp.maximum(dmax, inter)
            a = jnp.exp(dmax - m_row)
            w_state = jnp.exp(inter - m_row)
            qc = jnp.dot(q, c_ref[d].astype(BF16), preferred_element_type=F32)
            qn = jnp.dot(q, n_ref[d].astype(BF16), preferred_element_type=F32)
            num = a * sv[:, :HEAD_W] + w_state * qc
            den = a * sv[:, HEAD_W:] + w_state * qn
            hh = num / jnp.maximum(jnp.abs(den), jnp.exp(-m_row))
            if d:
                hb_ref[sl, :] = hh
            else:
                hf_ref[sl, :] = hh
            m_new = jnp.maximum(b_last + m, wmax)
            beta = jnp.exp(wmax - m_new)
            decay = jnp.exp(b_last + m - m_new)
            c_ref[d] = decay * c_ref[d] + beta * kv[:, :HEAD_W]
            n_ref[d] = decay * n_ref[d] + beta * kv[:, HEAD_W:]
            out.append(m_new)
        return tuple(out)

    m0 = jnp.full((1, LANES), NEG, F32)
    lax.fori_loop(0, n, recur, (m0, m0), unroll=2)

    y = _sigmoid(co_ref[0]) * _layer_norm_head(hf_ref[...] + hb_ref[...], gn_ref[...])
    o_ref[0] = y.astype(o_ref.dtype)


def _mlstm(za, zb, zg, gate_b, cwq, cwk, gain, B, S):
    n = S // CHUNK
    per_chunk = lambda rows, cols: pltpu.VMEM((2, n, rows, cols), F32)
    return pl.pallas_call(
        functools.partial(_mlstm_kernel, S=S),
        grid=(B, N_HEADS),
        in_specs=[
            pl.BlockSpec((1, S, HEAD_W), lambda b, h: (b, 0, 5 * N_HEADS + h)),
            pl.BlockSpec((1, S, HEAD_W), lambda b, h: (b, 0, 6 * N_HEADS + h)),
            pl.BlockSpec((1, S, HEAD_W), lambda b, h: (b, 0, 4 * N_HEADS + h)),
            pl.BlockSpec((1, S, LANES), lambda b, h: (b, 0, 0)),
            pl.BlockSpec((1, LANES), lambda b, h: (0, 0)),
            pl.BlockSpec((CONV_W, HEAD_W), lambda b, h: (0, h)),
            pl.BlockSpec((CONV_W, HEAD_W), lambda b, h: (0, h)),
            pl.BlockSpec((1, S, HEAD_W), lambda b, h: (b, 0, 7 * N_HEADS + h)),
            pl.BlockSpec((1, HEAD_W), lambda b, h: (0, h)),
        ],
        out_specs=pl.BlockSpec((1, S, HEAD_W), lambda b, h: (b, 0, h)),
        out_shape=jax.ShapeDtypeStruct((B, S, D_MODEL), BF16),
        scratch_shapes=[
            pltpu.VMEM((S, HEAD_W), BF16),
            pltpu.VMEM((n, HEAD_W, CHUNK), F32),
            pltpu.VMEM((n, 8, CHUNK), F32),
            pltpu.VMEM((S, 2 * HEAD_W), BF16),
            per_chunk(CHUNK, 2 * HEAD_W),
            per_chunk(HEAD_W, 2 * HEAD_W),
            per_chunk(CHUNK, LANES),
            per_chunk(CHUNK, LANES),
            per_chunk(8, LANES),
            pltpu.VMEM((S, HEAD_W), F32), pltpu.VMEM((S, HEAD_W), F32),
            pltpu.VMEM((2, HEAD_W, HEAD_W), F32), pltpu.VMEM((2, HEAD_W, HEAD_W), F32),
        ],
        compiler_params=_params(("parallel", "arbitrary")),
        name="mlstm",
    )(zb, zb, za, zg, gate_b, cwq, cwk, zb, gain)


def _merge_kernel(x_ref, ya_ref, yb_ref, yc_ref, bgt_ref, gb_ref, wa_ref, wb_ref, wc_ref, wo_ref,
                  g2_ref, wr_ref, xo_ref, xn_ref, lg_ref):
    D = D_MODEL
    gate = lambda j: _sigmoid(bgt_ref[:, j * D:(j + 1) * D] + gb_ref[j:j + 1, :])
    mixed = (gate(0) * jnp.dot(ya_ref[...], wa_ref[...], preferred_element_type=F32)
             + gate(1) * jnp.dot(yb_ref[...], wb_ref[...], preferred_element_type=F32)
             + gate(2) * jnp.dot(yc_ref[...], wc_ref[...], preferred_element_type=F32))
    x = x_ref[...] + jnp.dot(mixed.astype(BF16), wo_ref[...], preferred_element_type=F32)
    xo_ref[...] = x
    xn = _rms(x, g2_ref[...]).astype(BF16)
    xn_ref[...] = xn
    lg_ref[...] = jnp.dot(xn, wr_ref[...], preferred_element_type=F32)


def _merge(x, ya, yb, yc, zb, gate_b, wa, wb, wc, wo, g2, wr, tm):
    T, D = x.shape
    row = lambda i: (i, 0)
    const = lambda i: (0, 0)
    return pl.pallas_call(
        _merge_kernel,
        grid=(T // tm,),
        in_specs=[
            pl.BlockSpec((tm, D), row), pl.BlockSpec((tm, D), row), pl.BlockSpec((tm, D), row),
            pl.BlockSpec((tm, D), row), pl.BlockSpec((tm, 3 * D), row), pl.BlockSpec((3, D), const),
            pl.BlockSpec((D, D), const), pl.BlockSpec((D, D), const), pl.BlockSpec((D, D), const),
            pl.BlockSpec((D, D), const), pl.BlockSpec((1, D), const), pl.BlockSpec((D, LANES), const),
        ],
        out_specs=[pl.BlockSpec((tm, D), row), pl.BlockSpec((tm, D), row), pl.BlockSpec((tm, LANES), row)],
        out_shape=[jax.ShapeDtypeStruct((T, D), F32), jax.ShapeDtypeStruct((T, D), BF16),
                   jax.ShapeDtypeStruct((T, LANES), F32)],
        compiler_params=_params(("parallel",)),
        name="merge",
    )(x, ya, yb, yc, zb, gate_b, wa, wb, wc, wo, g2, wr)


def _router_kernel(lg_ref, tri_ref, aff_ref, pos_ref, *, cap):
    E = N_EXPERTS
    logits = lg_ref[0].T[:E]
    e = jnp.exp(logits - jnp.max(logits, axis=0, keepdims=True))
    aff = e / jnp.sum(e, axis=0, keepdims=True)
    key = pltpu.bitcast(aff, jnp.int32)

    def refine(i, thr):
        cand = thr | (jnp.int32(1) << (30 - i))
        cnt = jnp.sum(jnp.where(key >= cand, 1.0, 0.0), axis=1, keepdims=True)
        return jnp.where(cnt >= cap, cand, thr)

    thr = lax.fori_loop(0, 31, refine, jnp.zeros((E, 1), jnp.int32))
    above = key > thr
    tied = key == thr
    need = cap - jnp.sum(jnp.where(above, 1.0, 0.0), axis=1, keepdims=True)
    tri = tri_ref[...]
    tie_rank = jnp.dot(jnp.where(tied, 1.0, 0.0).astype(BF16), tri, preferred_element_type=F32)
    chosen = jnp.where(above, 1.0, jnp.where(tied, jnp.where(tie_rank < need, 1.0, 0.0), 0.0))
    slot = jnp.dot(chosen.astype(BF16), tri, preferred_element_type=F32)
    aff_ref[0] = aff
    pos_ref[0] = jnp.where(chosen > 0.0, slot, -1.0)


def _router(logits, tri, B, S, cap):
    out = jax.ShapeDtypeStruct((B, N_EXPERTS, S), F32)
    blk = pl.BlockSpec((1, N_EXPERTS, S), lambda b: (b, 0, 0))
    return pl.pallas_call(
        functools.partial(_router_kernel, cap=cap),
        grid=(B,),
        in_specs=[pl.BlockSpec((1, S, LANES), lambda b: (b, 0, 0)), pl.BlockSpec((S, S), lambda b: (0, 0))],
        out_specs=[blk, blk],
        out_shape=[out, out],
        compiler_params=_params(("parallel",)),
        name="router",
    )(logits, tri)


def _moe_kernel(xn_ref, pos_ref, aff_ref, wg_ref, wu_ref, wd_ref, o_ref,
                sel_ref, xe_ref, y_ref, gate_ref, *, S, cap, row_blk):
    e, f = pl.program_id(1), pl.program_id(2)

    @pl.when((e == 0) & (f == 0))
    def _():
        o_ref[...] = jnp.zeros_like(o_ref)

    @pl.when(f == 0)
    def _():
        slot = lax.broadcasted_iota(jnp.int32, (cap, S), 0).astype(F32)
        hit = slot == pos_ref[0, 0]
        sel = jnp.where(hit, 1.0, 0.0).astype(BF16)
        sel_ref[...] = sel
        xe_ref[...] = jnp.dot(sel, xn_ref[0], preferred_element_type=F32).astype(BF16)
        gate = jnp.sum(jnp.where(hit, aff_ref[0, 0], 0.0), axis=1, keepdims=True)
        gate_ref[...] = jnp.broadcast_to(gate, gate_ref.shape)
        y_ref[...] = jnp.zeros_like(y_ref)

    xe = xe_ref[...]
    hid = (_silu(jnp.dot(xe, wg_ref[0], preferred_element_type=F32))
           * jnp.dot(xe, wu_ref[0], preferred_element_type=F32))
    y_ref[...] += jnp.dot(hid.astype(BF16), wd_ref[0], preferred_element_type=F32)

    @pl.when(f == pl.num_programs(2) - 1)
    def _():
        y = (y_ref[...] * gate_ref[:, 0:1]).astype(BF16)
        for r in range(S // row_blk):
            rows = slice(r * row_blk, (r + 1) * row_blk)
            o_ref[0, rows, :] += lax.dot_general(sel_ref[:, rows], y, TN_DIMS, preferred_element_type=F32)


def _moe(xn, pos, aff, wg, wu, wd, B, S, cap, tf):
    D = D_MODEL
    row_blk = min(S, 512)
    return pl.pallas_call(
        functools.partial(_moe_kernel, S=S, cap=cap, row_blk=row_blk),
        grid=(B, N_EXPERTS, D_FF // tf),
        in_specs=[
            pl.BlockSpec((1, S, D), lambda b, e, f: (b, 0, 0)),
            pl.BlockSpec((1, 1, 1, S), lambda b, e, f: (b, e, 0, 0)),
            pl.BlockSpec((1, 1, 1, S), lambda b, e, f: (b, e, 0, 0)),
            pl.BlockSpec((1, D, tf), lambda b, e, f: (e, 0, f)),
            pl.BlockSpec((1, D, tf), lambda b, e, f: (e, 0, f)),
            pl.BlockSpec((1, tf, D), lambda b, e, f: (e, f, 0)),
        ],
        out_specs=pl.BlockSpec((1, S, D), lambda b, e, f: (b, 0, 0)),
        out_shape=jax.ShapeDtypeStruct((B, S, D), F32),
        scratch_shapes=[pltpu.VMEM((cap, S), BF16), pltpu.VMEM((cap, D), BF16),
                        pltpu.VMEM((cap, D), F32), pltpu.VMEM((cap, LANES), F32)],
        compiler_params=_params(("parallel", "arbitrary", "arbitrary")),
        name="moe",
    )(xn, pos, aff, wg, wu, wd)


def _final_norm_kernel(x_ref, r_ref, g_ref, o_ref):
    o_ref[...] = _rms(x_ref[...] + r_ref[...], g_ref[...])


def _final_norm(x, r, g, tm):
    T, D = x.shape
    row = pl.BlockSpec((tm, D), lambda i: (i, 0))
    return pl.pallas_call(
        _final_norm_kernel,
        grid=(T // tm,),
        in_specs=[row, row, pl.BlockSpec((1, D), lambda i: (0, 0))],
        out_specs=row,
        out_shape=jax.ShapeDtypeStruct((T, D), F32),
        compiler_params=_params(("parallel",)),
        name="final_norm",
    )(x, r, g)


def _rotary_tables(S):
    half = DKB // 2
    freqs = ROPE_BASE ** (-jnp.arange(half, dtype=F32) / half)
    ang = jnp.arange(S, dtype=F32)[:, None] * freqs[None, :]
    cos, sin = jnp.cos(ang), jnp.sin(ang)
    reps = LANES // DKB
    return jnp.tile(cos, (1, 2 * reps)), jnp.tile(jnp.concatenate([-sin, sin], axis=1), (1, reps))


def _pad_cols(a, width):
    return jnp.pad(a, ((0, 0), (0, width - a.shape[1])))


def kernel(x, t5_bias, attn_norm_g, w_in, branch_gate_b, diff_lambda, diff_norm_g, ret_decay_logit, ret_norm_g, mlstm_conv_w, mlstm_gate_b, mlstm_norm_g, w_branch_a, w_branch_b, w_branch_c, w_out, ffn_norm_g, w_router, w_exp_gate, w_exp_up, w_exp_down, final_norm_g):
    B, S, D = x.shape
    assert D == D_MODEL and S % CHUNK == 0
    T = B * S
    depth = w_in.shape[0]
    cap = CAPACITY_FACTOR * S // N_EXPERTS
    tm = min(1024, T)
    tq = min(1024, S)
    tf = D_FF // 2

    bias = _t5_bias(t5_bias.astype(F32), S, min(256, S))
    cos, sin = _rotary_tables(S)
    tri = (jnp.arange(S)[:, None] < jnp.arange(S)[None, :]).astype(BF16)

    x = x.reshape(T, D).astype(F32)
    ffn = None
    for l in range(depth):
        lam_init = 0.8 - 0.6 * math.exp(-0.3 * l)
        (w_aq, w_ak, w_av, w_bq, w_bk, w_bv, w_bg, w_cq, w_ck, w_cv, w_co, w_cgt, w_bgt) = jnp.split(
            w_in[l], _IN_OFFSETS, axis=-1)
        wa_cols = jnp.concatenate([w_aq * LOG2E, w_ak, w_av, w_bv, w_cv], axis=1).astype(BF16)
        wb_cols = jnp.concatenate([w_bgt, w_bq, w_bk, w_bg, w_cq, w_ck, w_co], axis=1).astype(BF16)
        wg_cols = _pad_cols(w_cgt, LANES).astype(BF16)
        g1 = attn_norm_g[l].reshape(1, D)

        if ffn is None:
            za = _norm_matmul(x, g1, wa_cols, BF16, tm, wa_cols.shape[1] // 2)
        else:
            x, za = _add_norm_matmul(x, ffn, g1, wa_cols, BF16, tm, wa_cols.shape[1] // 2)
        za = za.reshape(B, S, -1)
        zb = _norm_matmul(x, g1, wb_cols, F32, tm, 2 * D)
        zg = _norm_matmul(x, g1, wg_cols, F32, tm, LANES).reshape(B, S, LANES)
        zb3 = zb.reshape(B, S, -1)

        ya = _diff_attention(za, bias, diff_lambda[l].astype(F32), diff_norm_g[l].reshape(D, 1), lam_init, B, S, tq)
        dec = jnp.broadcast_to(ret_decay_logit[l].astype(F32).T[:, :, None], (N_HEADS, 2, LANES))
        yb = _retention(za, zb3, dec, cos, sin, ret_norm_g[l].reshape(1, D), B, S)
        gate_b = _pad_cols(mlstm_gate_b[l].astype(F32).reshape(1, GATE_COLS), LANES)
        cw = mlstm_conv_w[l].astype(F32)
        yc = _mlstm(za, zb3, zg, gate_b, cw[:, :D], cw[:, D:], mlstm_norm_g[l].reshape(1, D), B, S)

        x, xn, logits = _merge(
            x, ya.reshape(T, D), yb.reshape(T, D), yc.reshape(T, D), zb, branch_gate_b[l].astype(F32),
            w_branch_a[l].astype(BF16), w_branch_b[l].astype(BF16), w_branch_c[l].astype(BF16),
            w_out[l].astype(BF16), ffn_norm_g[l].reshape(1, D), _pad_cols(w_router[l], LANES).astype(BF16),
            min(256, T))

        aff, pos = _router(logits.reshape(B, S, LANES), tri, B, S, cap)
        ffn = _moe(xn.reshape(B, S, D), pos.reshape(B, N_EXPERTS, 1, S),
                   aff.reshape(B, N_EXPERTS, 1, S), w_exp_gate[l].astype(BF16), w_exp_up[l].astype(BF16),
                   w_exp_down[l].astype(BF16), B, S, cap, tf).reshape(T, D)
    return _final_norm(x, ffn, final_norm_g.reshape(1, D), tm).reshape(B, S, D)


_IN_WIDTHS = (N_HEADS * 2 * DA, N_HEADS * 2 * DA, N_HEADS * HEAD_W, N_HEADS * DKB, N_HEADS * DKB,
              N_HEADS * HEAD_W, N_HEADS * HEAD_W, N_HEADS * HEAD_W, N_HEADS * HEAD_W, N_HEADS * HEAD_W,
              N_HEADS * HEAD_W, GATE_COLS, 3 * D_MODEL)
_IN_OFFSETS = tuple(int(sum(_IN_WIDTHS[:i + 1])) for i in range(len(_IN_WIDTHS) - 1))
```

```python
import functools
import math

import jax
import jax.numpy as jnp
from jax import lax
from jax.experimental import pallas as pl
from jax.experimental.pallas import tpu as pltpu

F32 = jnp.float32
BF16 = jnp.bfloat16

D_MODEL = 1024
N_HEADS = 8
HEAD_W = 128
DA = 64
DKB = 64
CHUNK = 128
CONV_W = 3
N_EXPERTS = 16
D_FF = 2048
CAPACITY_FACTOR = 2
T5_BUCKETS = 32
T5_MAX_DIST = 128
ROPE_BASE = 10000.0
EPS = 1e-6
NEG = -1e30
GATE_COLS = 32
LOG2E = math.log2(math.e)
LANES = 128

VMEM_LIMIT = 56 * 1024 * 1024

NT_DIMS = (((1,), (1,)), ((), ()))
TN_DIMS = (((0,), (0,)), ((), ()))


def _params(sem):
    return pltpu.CompilerParams(dimension_semantics=sem, vmem_limit_bytes=VMEM_LIMIT)


def _log_sigmoid(x):
    return -(jnp.maximum(-x, 0.0) + jnp.log1p(jnp.exp(-jnp.abs(x))))


def _sigmoid(x):
    return 1.0 / (1.0 + jnp.exp(-x))


def _silu(x):
    return x * _sigmoid(x)


def _rms(x, g):
    return x * lax.rsqrt(jnp.mean(x * x, axis=-1, keepdims=True) + EPS) * g


def _layer_norm_head(h, gain):
    h = h - jnp.mean(h, axis=-1, keepdims=True)
    return h * lax.rsqrt(jnp.mean(h * h, axis=-1, keepdims=True) + EPS) * gain


def _norm_matmul_kernel(x_ref, g_ref, w_ref, o_ref, xn_ref):
    @pl.when(pl.program_id(1) == 0)
    def _():
        xn_ref[...] = _rms(x_ref[...], g_ref[...]).astype(BF16)

    o_ref[...] = jnp.dot(xn_ref[...], w_ref[...], preferred_element_type=F32).astype(o_ref.dtype)


def _norm_matmul(x, g, w, out_dtype, tm, tn):
    T, D = x.shape
    N = w.shape[1]
    return pl.pallas_call(
        _norm_matmul_kernel,
        grid=(T // tm, N // tn),
        in_specs=[
            pl.BlockSpec((tm, D), lambda i, j: (i, 0)),
            pl.BlockSpec((1, D), lambda i, j: (0, 0)),
            pl.BlockSpec((D, tn), lambda i, j: (0, j)),
        ],
        out_specs=pl.BlockSpec((tm, tn), lambda i, j: (i, j)),
        out_shape=jax.ShapeDtypeStruct((T, N), out_dtype),
        scratch_shapes=[pltpu.VMEM((tm, D), BF16)],
        compiler_params=_params(("parallel", "arbitrary")),
        name="norm_matmul",
    )(x, g, w)


def _norm_matmul2_kernel(x_ref, g_ref, w_ref, w2_ref, o_ref, o2_ref, xn_ref):
    @pl.when(pl.program_id(1) == 0)
    def _():
        xn = _rms(x_ref[...], g_ref[...]).astype(BF16)
        xn_ref[...] = xn
        o2_ref[...] = jnp.dot(xn, w2_ref[...], preferred_element_type=F32)

    o_ref[...] = jnp.dot(xn_ref[...], w_ref[...], preferred_element_type=F32).astype(o_ref.dtype)


def _norm_matmul2(x, g, w, w2, out_dtype, tm, tn):
    T, D = x.shape
    N, N2 = w.shape[1], w2.shape[1]
    return pl.pallas_call(
        _norm_matmul2_kernel,
        grid=(T // tm, N // tn),
        in_specs=[
            pl.BlockSpec((tm, D), lambda i, j: (i, 0)),
            pl.BlockSpec((1, D), lambda i, j: (0, 0)),
            pl.BlockSpec((D, tn), lambda i, j: (0, j)),
            pl.BlockSpec((D, N2), lambda i, j: (0, 0)),
        ],
        out_specs=[pl.BlockSpec((tm, tn), lambda i, j: (i, j)), pl.BlockSpec((tm, N2), lambda i, j: (i, 0))],
        out_shape=[jax.ShapeDtypeStruct((T, N), out_dtype), jax.ShapeDtypeStruct((T, N2), F32)],
        scratch_shapes=[pltpu.VMEM((tm, D), BF16)],
        compiler_params=_params(("parallel", "arbitrary")),
        name="norm_matmul2",
    )(x, g, w, w2)


def _add_norm_matmul_kernel(x_ref, r_ref, g_ref, w_ref, xs_ref, o_ref, xn_ref):
    @pl.when(pl.program_id(1) == 0)
    def _():
        xs = x_ref[...] + r_ref[...]
        xs_ref[...] = xs
        xn_ref[...] = _rms(xs, g_ref[...]).astype(BF16)

    o_ref[...] = jnp.dot(xn_ref[...], w_ref[...], preferred_element_type=F32).astype(o_ref.dtype)


def _add_norm_matmul(x, r, g, w, out_dtype, tm, tn):
    T, D = x.shape
    N = w.shape[1]
    row = pl.BlockSpec((tm, D), lambda i, j: (i, 0))
    return pl.pallas_call(
        _add_norm_matmul_kernel,
        grid=(T // tm, N // tn),
        in_specs=[row, row, pl.BlockSpec((1, D), lambda i, j: (0, 0)), pl.BlockSpec((D, tn), lambda i, j: (0, j))],
        out_specs=[row, pl.BlockSpec((tm, tn), lambda i, j: (i, j))],
        out_shape=[jax.ShapeDtypeStruct((T, D), F32), jax.ShapeDtypeStruct((T, N), out_dtype)],
        scratch_shapes=[pltpu.VMEM((tm, D), BF16)],
        compiler_params=_params(("parallel", "arbitrary")),
        name="add_norm_matmul",
    )(x, r, g, w)


def _t5_bias_kernel(tab_ref, bkt_ref, o_ref, *, tk, S):
    h = pl.program_id(0)
    ki = pl.program_id(1)
    bkt = bkt_ref[...]
    r = jnp.zeros((1, 2 * S), F32)
    for b in range(T5_BUCKETS):
        r = jnp.where(bkt == b, tab_ref[b, h] * LOG2E, r)
    x = jnp.broadcast_to(r, (tk, 2 * S))
    rolled = pltpu.roll(x, ki * tk + S, 1, stride=1, stride_axis=0)
    o_ref[0] = rolled[:, :S]


def _t5_bucket(rel):
    half = T5_BUCKETS // 2
    max_exact = half // 2
    offset = jnp.where(rel > 0, half, 0)
    n = jnp.abs(rel)
    large = max_exact + (jnp.log(jnp.maximum(n, 1).astype(F32) / max_exact)
                         / math.log(T5_MAX_DIST / max_exact) * (half - max_exact)).astype(jnp.int32)
    large = jnp.minimum(large, half - 1)
    return offset + jnp.where(n < max_exact, n, large)


def _t5_bias(table, S, tk):
    rel = S - jnp.arange(2 * S, dtype=jnp.int32)
    bkt = _t5_bucket(rel).astype(jnp.int32).reshape(1, 2 * S)
    return pl.pallas_call(
        functools.partial(_t5_bias_kernel, tk=tk, S=S),
        grid=(N_HEADS, S // tk),
        in_specs=[
            pl.BlockSpec(memory_space=pltpu.SMEM),
            pl.BlockSpec((1, 2 * S), lambda h, k: (0, 0)),
        ],
        out_specs=pl.BlockSpec((1, tk, S), lambda h, k: (h, k, 0)),
        out_shape=jax.ShapeDtypeStruct((N_HEADS, S, S), F32),
        compiler_params=_params(("parallel", "arbitrary")),
        name="t5_bias",
    )(table, bkt)


ONES_ROWS = 16


KEY_BLK = 512
BF16_ROWS = 16
PREP_GROUP = 8


def _attn_kernel(dl_ref, q_ref, k_ref, v_ref, b_ref, g_ref, o_ref, vt_ref, *, lam_init):
    @pl.when(pl.program_id(2) == 0)
    def _():
        vt_ref[:HEAD_W, :] = v_ref[0].astype(F32).T.astype(BF16)
        vt_ref[HEAD_W:, :] = jnp.ones((ONES_ROWS, vt_ref.shape[1]), BF16)

    dl = dl_ref[...]
    lam = (jnp.exp(jnp.sum(dl[0:1] * dl[1:2], axis=-1, keepdims=True))
           - jnp.exp(jnp.sum(dl[2:3] * dl[3:4], axis=-1, keepdims=True)) + lam_init)
    q = q_ref[0] * (DA ** -0.5)
    lane = lax.broadcasted_iota(jnp.int32, q.shape, 1)
    q_halves = (jnp.where(lane < DA, q, jnp.zeros_like(q)), jnp.where(lane >= DA, q, jnp.zeros_like(q)))
    key_blk = min(KEY_BLK, k_ref.shape[1])
    n_blk = k_ref.shape[1] // key_blk

    def score_block(half, j):
        rows = slice(j * key_blk, (j + 1) * key_blk)
        s = lax.dot_general(k_ref[0, rows, :], q_halves[half], NT_DIMS, preferred_element_type=F32)
        return (s + b_ref[0, rows, :]).astype(BF16)

    m = [None, None]
    acc = [None, None]
    scores = [None, None]
    for j in range(n_blk + 1):
        pending = scores
        scores = [score_block(half, j) for half in range(2)] if j < n_blk else [None, None]
        if j == 0:
            continue
        rows = slice((j - 1) * key_blk, j * key_blk)
        for half in range(2):
            s = pending[half]
            tile_max = jnp.max(s.reshape(key_blk // BF16_ROWS, BF16_ROWS, s.shape[1]), axis=0)
            m_blk = jnp.max(tile_max.astype(F32), axis=0, keepdims=True)
            m_new = m_blk if m[half] is None else jnp.maximum(m[half], m_blk)
            e = jnp.exp2(s - m_new.astype(BF16))
            pv = jnp.dot(vt_ref[:, rows], e, preferred_element_type=F32)
            acc[half] = pv if acc[half] is None else acc[half] * jnp.exp2(m[half] - m_new) + pv
            m[half] = m_new

    o = (acc[0][:HEAD_W] / acc[0][HEAD_W:HEAD_W + 1]
         - lam * (acc[1][:HEAD_W] / acc[1][HEAD_W:HEAD_W + 1]))
    y = o * lax.rsqrt(jnp.mean(o * o, axis=0, keepdims=True) + EPS) * g_ref[...] * (1.0 - lam_init)
    o_ref[0] = y.T.astype(o_ref.dtype)


def _diff_attention(za, bias_t, dl, gain_col, lam_init, B, S, tq):
    return pl.pallas_call(
        functools.partial(_attn_kernel, lam_init=lam_init),
        grid=(N_HEADS, B, S // tq),
        in_specs=[
            pl.BlockSpec((4, DA), lambda h, b, q: (0, 0)),
            pl.BlockSpec((1, tq, HEAD_W), lambda h, b, q: (b, q, h)),
            pl.BlockSpec((1, S, HEAD_W), lambda h, b, q: (b, 0, N_HEADS + h)),
            pl.BlockSpec((1, S, HEAD_W), lambda h, b, q: (b, 0, 2 * N_HEADS + h)),
            pl.BlockSpec((1, S, tq), lambda h, b, q: (h, 0, q)),
            pl.BlockSpec((HEAD_W, 1), lambda h, b, q: (h, 0)),
        ],
        out_specs=pl.BlockSpec((1, tq, HEAD_W), lambda h, b, q: (b, q, h)),
        out_shape=jax.ShapeDtypeStruct((B, S, D_MODEL), BF16),
        scratch_shapes=[pltpu.VMEM((HEAD_W + ONES_ROWS, S), BF16)],
        compiler_params=_params(("parallel", "parallel", "arbitrary")),
        name="diff_attention",
    )(dl, za, za, za, bias_t, gain_col)


def _retention_kernel(dec_ref, q_ref, k_ref, v_ref, bg_ref, cos_ref, sin_ref, g_ref, o_ref,
                      qs_ref, kt_ref, acc_ref, kvf_ref, kvb_ref, r_ref, *, S):
    L = CHUNK
    n = S // L
    lane = lax.broadcasted_iota(jnp.int32, (S, LANES), 1)
    first_half = (lane & (DKB // 2)) == 0
    cos = cos_ref[...]
    sin = sin_ref[...]

    def rotary(t):
        partner = jnp.where(first_half, pltpu.roll(t, LANES - DKB // 2, 1), pltpu.roll(t, DKB // 2, 1))
        return t * cos + partner * sin

    qs_ref[...] = rotary(q_ref[0])
    ks = rotary(k_ref[0]) * (DKB ** -0.5)
    for c in range(n):
        kt_ref[c] = ks[c * L:(c + 1) * L].T

    row = lax.broadcasted_iota(jnp.int32, (L, LANES), 0).astype(F32)
    col = lax.broadcasted_iota(jnp.int32, (L, LANES), 1).astype(F32)
    lane_l = lax.broadcasted_iota(jnp.int32, (L, LANES), 1)
    pos_row = col[0:1]
    diff = row - col
    heads = []
    for j in range(2):
        lg = _log_sigmoid(dec_ref[j])
        lgf, lgb = lg[0:1], lg[1:2]
        heads.append(dict(
            mask=(lane_l // DKB) == j,
            dmat=(jnp.where(diff >= 0, jnp.exp(lgf * jnp.maximum(diff, 0.0)), 0.0)
                  + jnp.where(diff <= 0, jnp.exp(lgb * jnp.maximum(-diff, 0.0)), 0.0)),
            xi_f=jnp.exp(lgf * (row + 1.0)), xi_b=jnp.exp(lgb * (L - row)),
            zeta_f=jnp.exp(lgf * (L - 1.0 - pos_row)), zeta_b=jnp.exp(lgb * pos_row),
            decay_f=jnp.exp(lgf * L), decay_b=jnp.exp(lgb * L)))

    def prepare(c, carry):
        sl = pl.ds(pl.multiple_of(c * L, L), L)
        qc = qs_ref[sl, :]
        kt = kt_ref[c]
        ktb = kt.astype(BF16)
        for j, hd in enumerate(heads):
            vj = v_ref[0, sl, j * HEAD_W:(j + 1) * HEAD_W]
            qj = jnp.where(hd["mask"], qc, 0.0).astype(BF16)
            inner = jnp.dot(qj, ktb, preferred_element_type=F32) * hd["dmat"]
            acc_ref[j, sl, :] = jnp.dot(inner.astype(BF16), vj, preferred_element_type=F32)
            kvf_ref[j, c] = jnp.dot((kt * hd["zeta_f"]).astype(BF16), vj, preferred_element_type=F32)
            kvb_ref[j, c] = jnp.dot((kt * hd["zeta_b"]).astype(BF16), vj, preferred_element_type=F32)
        return carry

    lax.fori_loop(0, n, prepare, 0, unroll=2)
    r_ref[...] = jnp.zeros_like(r_ref)

    def recur(i, carry):
        for j, hd in enumerate(heads):
            for rev in (False, True):
                c = (n - 1 - i) if rev else i
                sl = pl.ds(pl.multiple_of(c * L, L), L)
                xi, decay, kv_ref = ((hd["xi_b"], hd["decay_b"], kvb_ref) if rev
                                     else (hd["xi_f"], hd["decay_f"], kvf_ref))
                r = r_ref[2 * j + rev]
                qj = (jnp.where(hd["mask"], qs_ref[sl, :], 0.0) * xi).astype(BF16)
                acc_ref[j, sl, :] += jnp.dot(qj, r.astype(BF16), preferred_element_type=F32)
                r_ref[2 * j + rev] = decay * r + kv_ref[j, c]
        return carry

    lax.fori_loop(0, n, recur, 0, unroll=2)
    for j in range(2):
        cols = slice(j * HEAD_W, (j + 1) * HEAD_W)
        y = _silu(bg_ref[0, :, cols]) * _layer_norm_head(acc_ref[j], g_ref[:, cols])
        o_ref[0, :, cols] = y.astype(o_ref.dtype)


def _retention(za, zb, dec, cos, sin, gain, B, S):
    qk_blk = 3 * N_HEADS
    pair_w = 2 * HEAD_W
    n = S // CHUNK
    return pl.pallas_call(
        functools.partial(_retention_kernel, S=S),
        grid=(B, N_HEADS // 2),
        in_specs=[
            pl.BlockSpec((2, 2, LANES), lambda b, p: (p, 0, 0)),
            pl.BlockSpec((1, S, LANES), lambda b, p: (b, 0, qk_blk + p)),
            pl.BlockSpec((1, S, LANES), lambda b, p: (b, 0, qk_blk + N_HEADS // 2 + p)),
            pl.BlockSpec((1, S, pair_w), lambda b, p: (b, 0, 3 * N_HEADS // 2 + p)),
            pl.BlockSpec((1, S, pair_w), lambda b, p: (b, 0, 4 * N_HEADS // 2 + p)),
            pl.BlockSpec((S, LANES), lambda b, p: (0, 0)),
            pl.BlockSpec((S, LANES), lambda b, p: (0, 0)),
            pl.BlockSpec((1, pair_w), lambda b, p: (0, p)),
        ],
        out_specs=pl.BlockSpec((1, S, pair_w), lambda b, p: (b, 0, p)),
        out_shape=jax.ShapeDtypeStruct((B, S, D_MODEL), BF16),
        scratch_shapes=[pltpu.VMEM((S, LANES), F32), pltpu.VMEM((n, LANES, CHUNK), F32),
                        pltpu.VMEM((2, S, HEAD_W), F32),
                        pltpu.VMEM((2, n, LANES, HEAD_W), F32), pltpu.VMEM((2, n, LANES, HEAD_W), F32),
                        pltpu.VMEM((4, LANES, HEAD_W), F32)],
        compiler_params=_params(("parallel", "arbitrary")),
        name="retention",
    )(dec, zb, zb, za, zb, cos, sin, gain)


def _mlstm_kernel(q_ref, k_ref, v_ref, gt_ref, gb_ref, cwq_ref, cwk_ref, co_ref, gn_ref, o_ref,
                  qs_ref, kt_ref, gsc_ref, va_ref, sv_ref, kv_ref, bcb_ref, dmx_ref, sc_ref, hf_ref, hb_ref,
                  c_ref, n_ref, *, S):
    L = CHUNK
    n = S // L
    h = pl.program_id(1)

    rows = lax.broadcasted_iota(jnp.int32, (S, HEAD_W), 0)

    def conv_silu(u, w):
        prev = jnp.where(rows == 0, 0.0, pltpu.roll(u, 1, 0))
        nxt = jnp.where(rows == S - 1, 0.0, pltpu.roll(u, S - 1, 0))
        return _silu(w[0:1] * prev + w[1:2] * u + w[2:3] * nxt)

    qs_ref[...] = conv_silu(q_ref[0], cwq_ref[...]).astype(BF16)
    va_ref[:, :HEAD_W] = v_ref[0]
    va_ref[:, HEAD_W:] = jnp.ones_like(v_ref[0])

    ks = conv_silu(k_ref[0], cwk_ref[...]) * (HEAD_W ** -0.5)
    gates = gt_ref[0] + gb_ref[...]
    gsub = lax.broadcasted_iota(jnp.int32, (GATE_COLS, L), 0)
    for c in range(n):
        kt_ref[c] = ks[c * L:(c + 1) * L].T
        gt = gates[c * L:(c + 1) * L].T[:GATE_COLS]
        pick = lambda j: jnp.sum(jnp.where(gsub == j, gt, 0.0), axis=0, keepdims=True)
        gsc_ref[c] = jnp.concatenate(
            [pick(h), _log_sigmoid(pick(N_HEADS + h)), pick(2 * N_HEADS + h),
             _log_sigmoid(pick(3 * N_HEADS + h)), jnp.zeros((4, L), F32)], axis=0)

    ri = lax.broadcasted_iota(jnp.int32, (L, L), 0)
    ci = lax.broadcasted_iota(jnp.int32, (L, L), 1)
    eye = ri == ci
    ones_rhs = jnp.ones((2 * L, LANES), BF16)

    causal = ((ci <= ri), (ci >= ri))
    grp = math.gcd(n, PREP_GROUP)
    units = [(u, d) for u in range(grp) for d in range(2)]

    def prepare(i, carry):
        cs = [i * grp + u for u in range(grp)]
        sls = [pl.ds(pl.multiple_of(c * L, L), L) for c in cs]
        kts = [kt_ref[c] for c in cs]
        gs = [gsc_ref[c] for c in cs]
        vas = [va_ref[sl, :] for sl in sls]
        qks = [jnp.dot(qs_ref[sl, :], kt.astype(BF16), preferred_element_type=F32) for sl, kt in zip(sls, kts)]
        b_cbs = {}
        for u, d in units:
            lfm = jnp.where(causal[d], gs[u][2 * d + 1:2 * d + 2], 0.0)
            hi = lfm.astype(BF16)
            lo = (lfm - hi.astype(F32)).astype(BF16)
            b_cbs[u, d] = jnp.dot(jnp.concatenate([hi, lo], axis=1), ones_rhs, preferred_element_type=F32)
        s0s, ktws = {}, {}
        for u, d in units:
            b_cb = b_cbs[u, d]
            ii = gs[u][2 * d:2 * d + 1]
            b_row = jnp.sum(jnp.where(eye, b_cb, 0.0), axis=0, keepdims=True)
            b_last = b_cb[0:1] if d else b_cb[L - 1:L]
            r_row = ii - b_row
            dlog = jnp.where(causal[d], b_cb + r_row, NEG)
            dmax = jnp.max(dlog, axis=1, keepdims=True)
            s0s[u, d] = (qks[u] * jnp.exp(dlog - dmax)).astype(BF16)
            wlog = r_row + b_last[:, 0:1]
            wmax = jnp.max(wlog, axis=1, keepdims=True)
            ktws[u, d] = (kts[u] * jnp.exp(wlog - wmax)).astype(BF16)
            bcb_ref[d, cs[u]] = b_cb
            dmx_ref[d, cs[u]] = jnp.broadcast_to(dmax, (L, LANES))
            sc_ref[d, cs[u]] = jnp.concatenate(
                [b_last, jnp.broadcast_to(wmax, (1, LANES)), jnp.zeros((6, LANES), F32)], axis=0)
        for u, d in units:
            sv_ref[d, cs[u]] = jnp.dot(s0s[u, d], vas[u], preferred_element_type=F32)
        for u, d in units:
            kv_ref[d, cs[u]] = jnp.dot(ktws[u, d], vas[u], preferred_element_type=F32)
        return carry

    lax.fori_loop(0, n // grp, prepare, 0)

    c_ref[...] = jnp.zeros_like(c_ref)
    n_ref[...] = jnp.zeros_like(n_ref)

    def recur(i, ms):
        out = []
        for d in range(2):
            c = (n - 1 - i) if d else i
            m = ms[d]
            sl = pl.ds(pl.multiple_of(c * L, L), L)
            q = qs_ref[sl, :]
            b_cb, dmax, sv, kv, sc = bcb_ref[d, c], dmx_ref[d, c], sv_ref[d, c], kv_ref[d, c], sc_ref[d, c]
            b_last, wmax = sc[0:1], sc[1:2]
            inter = b_cb + m
            m_row = jnp.maximum(dmax, inter)
            a = jnp.exp(dmax - m_row)
            w_state = jnp.exp(inter - m_row)
            qc = jnp.dot(q, c_ref[d].astype(BF16), preferred_element_type=F32)
            qn = jnp.dot(q, n_ref[d].astype(BF16), preferred_element_type=F32)
            num = a * sv[:, :HEAD_W] + w_state * qc
            den = a * sv[:, HEAD_W:] + w_state * qn
            hh = num / jnp.maximum(jnp.abs(den), jnp.exp(-m_row))
            if d:
                hb_ref[sl, :] = hh
            else:
                hf_ref[sl, :] = hh
            m_new = jnp.maximum(b_last + m, wmax)
            beta = jnp.exp(wmax - m_new)
            decay = jnp.exp(b_last + m - m_new)
            c_ref[d] = decay * c_ref[d] + beta * kv[:, :HEAD_W]
            n_ref[d] = decay * n_ref[d] + beta * kv[:, HEAD_W:]
            out.append(m_new)
        return tuple(out)

    m0 = jnp.full((1, LANES), NEG, F32)
    lax.fori_loop(0, n, recur, (m0, m0), unroll=2)

    y = _sigmoid(co_ref[0]) * _layer_norm_head(hf_ref[...] + hb_ref[...], gn_ref[...])
    o_ref[0] = y.astype(o_ref.dtype)


def _mlstm(za, zb, zg, gate_b, cwq, cwk, gain, B, S):
    n = S // CHUNK
    per_chunk = lambda rows, cols: pltpu.VMEM((2, n, rows, cols), F32)
    return pl.pallas_call(
        functools.partial(_mlstm_kernel, S=S),
        grid=(B, N_HEADS),
        in_specs=[
            pl.BlockSpec((1, S, HEAD_W), lambda b, h: (b, 0, 5 * N_HEADS + h)),
            pl.BlockSpec((1, S, HEAD_W), lambda b, h: (b, 0, 6 * N_HEADS + h)),
            pl.BlockSpec((1, S, HEAD_W), lambda b, h: (b, 0, 4 * N_HEADS + h)),
            pl.BlockSpec((1, S, LANES), lambda b, h: (b, 0, 0)),
            pl.BlockSpec((1, LANES), lambda b, h: (0, 0)),
            pl.BlockSpec((CONV_W, HEAD_W), lambda b, h: (0, h)),
            pl.BlockSpec((CONV_W, HEAD_W), lambda b, h: (0, h)),
            pl.BlockSpec((1, S, HEAD_W), lambda b, h: (b, 0, 7 * N_HEADS + h)),
            pl.BlockSpec((1, HEAD_W), lambda b, h: (0, h)),
        ],
        out_specs=pl.BlockSpec((1, S, HEAD_W), lambda b, h: (b, 0, h)),
        out_shape=jax.ShapeDtypeStruct((B, S, D_MODEL), BF16),
        scratch_shapes=[
            pltpu.VMEM((S, HEAD_W), BF16),
            pltpu.VMEM((n, HEAD_W, CHUNK), F32),
            pltpu.VMEM((n, 8, CHUNK), F32),
            pltpu.VMEM((S, 2 * HEAD_W), BF16),
            per_chunk(CHUNK, 2 * HEAD_W),
            per_chunk(HEAD_W, 2 * HEAD_W),
            per_chunk(CHUNK, LANES),
            per_chunk(CHUNK, LANES),
            per_chunk(8, LANES),
            pltpu.VMEM((S, HEAD_W), F32), pltpu.VMEM((S, HEAD_W), F32),
            pltpu.VMEM((2, HEAD_W, HEAD_W), F32), pltpu.VMEM((2, HEAD_W, HEAD_W), F32),
        ],
        compiler_params=_params(("parallel", "arbitrary")),
        name="mlstm",
    )(zb, zb, za, zg, gate_b, cwq, cwk, zb, gain)


def _merge_kernel(x_ref, ya_ref, yb_ref, yc_ref, bgt_ref, gb_ref, wa_ref, wb_ref, wc_ref, wo_ref,
                  g2_ref, wr_ref, xo_ref, xn_ref, lg_ref):
    D = D_MODEL
    gate = lambda j: _sigmoid(bgt_ref[:, j * D:(j + 1) * D] + gb_ref[j:j + 1, :])
    mixed = (gate(0) * jnp.dot(ya_ref[...], wa_ref[...], preferred_element_type=F32)
             + gate(1) * jnp.dot(yb_ref[...], wb_ref[...], preferred_element_type=F32)
             + gate(2) * jnp.dot(yc_ref[...], wc_ref[...], preferred_element_type=F32))
    x = x_ref[...] + jnp.dot(mixed.astype(BF16), wo_ref[...], preferred_element_type=F32)
    xo_ref[...] = x
    xn = _rms(x, g2_ref[...]).astype(BF16)
    xn_ref[...] = xn
    lg_ref[...] = jnp.dot(xn, wr_ref[...], preferred_element_type=F32)


def _merge(x, ya, yb, yc, zb, gate_b, wa, wb, wc, wo, g2, wr, tm):
    T, D = x.shape
    row = lambda i: (i, 0)
    const = lambda i: (0, 0)
    return pl.pallas_call(
        _merge_kernel,
        grid=(T // tm,),
        in_specs=[
            pl.BlockSpec((tm, D), row), pl.BlockSpec((tm, D), row), pl.BlockSpec((tm, D), row),
            pl.BlockSpec((tm, D), row), pl.BlockSpec((tm, 3 * D), row), pl.BlockSpec((3, D), const),
            pl.BlockSpec((D, D), const), pl.BlockSpec((D, D), const), pl.BlockSpec((D, D), const),
            pl.BlockSpec((D, D), const), pl.BlockSpec((1, D), const), pl.BlockSpec((D, LANES), const),
        ],
        out_specs=[pl.BlockSpec((tm, D), row), pl.BlockSpec((tm, D), row), pl.BlockSpec((tm, LANES), row)],
        out_shape=[jax.ShapeDtypeStruct((T, D), F32), jax.ShapeDtypeStruct((T, D), BF16),
                   jax.ShapeDtypeStruct((T, LANES), F32)],
        compiler_params=_params(("parallel",)),
        name="merge",
    )(x, ya, yb, yc, zb, gate_b, wa, wb, wc, wo, g2, wr)


def _router_kernel(lg_ref, tri_ref, aff_ref, pos_ref, *, cap):
    E = N_EXPERTS
    logits = lg_ref[0].T[:E]
    e = jnp.exp(logits - jnp.max(logits, axis=0, keepdims=True))
    aff = e / jnp.sum(e, axis=0, keepdims=True)
    key = pltpu.bitcast(aff, jnp.int32)

    def refine(i, thr):
        cand = thr | (jnp.int32(1) << (30 - i))
        cnt = jnp.sum(jnp.where(key >= cand, 1.0, 0.0), axis=1, keepdims=True)
        return jnp.where(cnt >= cap, cand, thr)

    thr = lax.fori_loop(0, 31, refine, jnp.zeros((E, 1), jnp.int32))
    above = key > thr
    tied = key == thr
    need = cap - jnp.sum(jnp.where(above, 1.0, 0.0), axis=1, keepdims=True)
    tri = tri_ref[...]
    tie_rank = jnp.dot(jnp.where(tied, 1.0, 0.0).astype(BF16), tri, preferred_element_type=F32)
    chosen = jnp.where(above, 1.0, jnp.where(tied, jnp.where(tie_rank < need, 1.0, 0.0), 0.0))
    slot = jnp.dot(chosen.astype(BF16), tri, preferred_element_type=F32)
    aff_ref[0] = aff
    pos_ref[0] = jnp.where(chosen > 0.0, slot, -1.0)


def _router(logits, tri, B, S, cap):
    out = jax.ShapeDtypeStruct((B, N_EXPERTS, S), F32)
    blk = pl.BlockSpec((1, N_EXPERTS, S), lambda b: (b, 0, 0))
    return pl.pallas_call(
        functools.partial(_router_kernel, cap=cap),
        grid=(B,),
        in_specs=[pl.BlockSpec((1, S, LANES), lambda b: (b, 0, 0)), pl.BlockSpec((S, S), lambda b: (0, 0))],
        out_specs=[blk, blk],
        out_shape=[out, out],
        compiler_params=_params(("parallel",)),
        name="router",
    )(logits, tri)


def _moe_kernel(xn_ref, pos_ref, aff_ref, wg_ref, wu_ref, wd_ref, o_ref,
                sel_ref, xe_ref, y_ref, gate_ref, *, S, cap, row_blk):
    e, f = pl.program_id(1), pl.program_id(2)

    @pl.when((e == 0) & (f == 0))
    def _():
        o_ref[...] = jnp.zeros_like(o_ref)

    @pl.when(f == 0)
    def _():
        slot = lax.broadcasted_iota(jnp.int32, (cap, S), 0).astype(F32)
        hit = slot == pos_ref[0, 0]
        sel = jnp.where(hit, 1.0, 0.0).astype(BF16)
        sel_ref[...] = sel
        xe_ref[...] = jnp.dot(sel, xn_ref[0], preferred_element_type=F32).astype(BF16)
        gate = jnp.sum(jnp.where(hit, aff_ref[0, 0], 0.0), axis=1, keepdims=True)
        gate_ref[...] = jnp.broadcast_to(gate, gate_ref.shape)
        y_ref[...] = jnp.zeros_like(y_ref)

    xe = xe_ref[...]
    hid = (_silu(jnp.dot(xe, wg_ref[0], preferred_element_type=F32))
           * jnp.dot(xe, wu_ref[0], preferred_element_type=F32))
    y_ref[...] += jnp.dot(hid.astype(BF16), wd_ref[0], preferred_element_type=F32)

    @pl.when(f == pl.num_programs(2) - 1)
    def _():
        y = (y_ref[...] * gate_ref[:, 0:1]).astype(BF16)
        for r in range(S // row_blk):
            rows = slice(r * row_blk, (r + 1) * row_blk)
            o_ref[0, rows, :] += lax.dot_general(sel_ref[:, rows], y, TN_DIMS, preferred_element_type=F32)


def _moe(xn, pos, aff, wg, wu, wd, B, S, cap, tf):
    D = D_MODEL
    row_blk = min(S, 512)
    return pl.pallas_call(
        functools.partial(_moe_kernel, S=S, cap=cap, row_blk=row_blk),
        grid=(B, N_EXPERTS, D_FF // tf),
        in_specs=[
            pl.BlockSpec((1, S, D), lambda b, e, f: (b, 0, 0)),
            pl.BlockSpec((1, 1, 1, S), lambda b, e, f: (b, e, 0, 0)),
            pl.BlockSpec((1, 1, 1, S), lambda b, e, f: (b, e, 0, 0)),
            pl.BlockSpec((1, D, tf), lambda b, e, f: (e, 0, f)),
            pl.BlockSpec((1, D, tf), lambda b, e, f: (e, 0, f)),
            pl.BlockSpec((1, tf, D), lambda b, e, f: (e, f, 0)),
        ],
        out_specs=pl.BlockSpec((1, S, D), lambda b, e, f: (b, 0, 0)),
        out_shape=jax.ShapeDtypeStruct((B, S, D), F32),
        scratch_shapes=[pltpu.VMEM((cap, S), BF16), pltpu.VMEM((cap, D), BF16),
                        pltpu.VMEM((cap, D), F32), pltpu.VMEM((cap, LANES), F32)],
        compiler_params=_params(("parallel", "arbitrary", "arbitrary")),
        name="moe",
    )(xn, pos, aff, wg, wu, wd)


def _final_norm_kernel(x_ref, r_ref, g_ref, o_ref):
    o_ref[...] = _rms(x_ref[...] + r_ref[...], g_ref[...])


def _final_norm(x, r, g, tm):
    T, D = x.shape
    row = pl.BlockSpec((tm, D), lambda i: (i, 0))
    return pl.pallas_call(
        _final_norm_kernel,
        grid=(T // tm,),
        in_specs=[row, row, pl.BlockSpec((1, D), lambda i: (0, 0))],
        out_specs=row,
        out_shape=jax.ShapeDtypeStruct((T, D), F32),
        compiler_params=_params(("parallel",)),
        name="final_norm",
    )(x, r, g)


def _rotary_tables(S):
    half = DKB // 2
    freqs = ROPE_BASE ** (-jnp.arange(half, dtype=F32) / half)
    ang = jnp.arange(S, dtype=F32)[:, None] * freqs[None, :]
    cos, sin = jnp.cos(ang), jnp.sin(ang)
    reps = LANES // DKB
    return jnp.tile(cos, (1, 2 * reps)), jnp.tile(jnp.concatenate([-sin, sin], axis=1), (1, reps))


def _pad_cols(a, width):
    return jnp.pad(a, ((0, 0), (0, width - a.shape[1])))


def kernel(x, t5_bias, attn_norm_g, w_in, branch_gate_b, diff_lambda, diff_norm_g, ret_decay_logit, ret_norm_g, mlstm_conv_w, mlstm_gate_b, mlstm_norm_g, w_branch_a, w_branch_b, w_branch_c, w_out, ffn_norm_g, w_router, w_exp_gate, w_exp_up, w_exp_down, final_norm_g):
    B, S, D = x.shape
    assert D == D_MODEL and S % CHUNK == 0
    T = B * S
    depth = w_in.shape[0]
    cap = CAPACITY_FACTOR * S // N_EXPERTS
    tm = min(1024, T)
    tq = min(1024, S)
    tf = D_FF // 2

    bias = _t5_bias(t5_bias.astype(F32), S, min(256, S))
    cos, sin = _rotary_tables(S)
    tri = (jnp.arange(S)[:, None] < jnp.arange(S)[None, :]).astype(BF16)

    x = x.reshape(T, D).astype(F32)
    ffn = None
    for l in range(depth):
        lam_init = 0.8 - 0.6 * math.exp(-0.3 * l)
        (w_aq, w_ak, w_av, w_bq, w_bk, w_bv, w_bg, w_cq, w_ck, w_cv, w_co, w_cgt, w_bgt) = jnp.split(
            w_in[l], _IN_OFFSETS, axis=-1)
        wa_cols = jnp.concatenate([w_aq * LOG2E, w_ak, w_av, w_bv, w_cv], axis=1).astype(BF16)
        wb_cols = jnp.concatenate([w_bgt, w_bq, w_bk, w_bg, w_cq, w_ck, w_co], axis=1).astype(BF16)
        wg_cols = _pad_cols(w_cgt, LANES).astype(BF16)
        g1 = attn_norm_g[l].reshape(1, D)

        if ffn is None:
            za = _norm_matmul(x, g1, wa_cols, BF16, tm, wa_cols.shape[1] // 2)
        else:
            x, za = _add_norm_matmul(x, ffn, g1, wa_cols, BF16, tm, wa_cols.shape[1] // 2)
        za = za.reshape(B, S, -1)
        zb, zg = _norm_matmul2(x, g1, wb_cols, wg_cols, F32, tm, 2 * D)
        zg = zg.reshape(B, S, LANES)
        zb3 = zb.reshape(B, S, -1)

        ya = _diff_attention(za, bias, diff_lambda[l].astype(F32), diff_norm_g[l].reshape(D, 1), lam_init, B, S, tq)
        dec = jnp.broadcast_to(ret_decay_logit[l].astype(F32).T[:, :, None], (N_HEADS, 2, LANES))
        yb = _retention(za, zb3, dec, cos, sin, ret_norm_g[l].reshape(1, D), B, S)
        gate_b = _pad_cols(mlstm_gate_b[l].astype(F32).reshape(1, GATE_COLS), LANES)
        cw = mlstm_conv_w[l].astype(F32)
        yc = _mlstm(za, zb3, zg, gate_b, cw[:, :D], cw[:, D:], mlstm_norm_g[l].reshape(1, D), B, S)

        x, xn, logits = _merge(
            x, ya.reshape(T, D), yb.reshape(T, D), yc.reshape(T, D), zb, branch_gate_b[l].astype(F32),
            w_branch_a[l].astype(BF16), w_branch_b[l].astype(BF16), w_branch_c[l].astype(BF16),
            w_out[l].astype(BF16), ffn_norm_g[l].reshape(1, D), _pad_cols(w_router[l], LANES).astype(BF16),
            min(256, T))

        aff, pos = _router(logits.reshape(B, S, LANES), tri, B, S, cap)
        ffn = _moe(xn.reshape(B, S, D), pos.reshape(B, N_EXPERTS, 1, S),
                   aff.reshape(B, N_EXPERTS, 1, S), w_exp_gate[l].astype(BF16), w_exp_up[l].astype(BF16),
                   w_exp_down[l].astype(BF16), B, S, cap, tf).reshape(T, D)
    return _final_norm(x, ffn, final_norm_g.reshape(1, D), tm).reshape(B, S, D)


_IN_WIDTHS = (N_HEADS * 2 * DA, N_HEADS * 2 * DA, N_HEADS * HEAD_W, N_HEADS * DKB, N_HEADS * DKB,
              N_HEADS * HEAD_W, N_HEADS * HEAD_W, N_HEADS * HEAD_W, N_HEADS * HEAD_W, N_HEADS * HEAD_W,
              N_HEADS * HEAD_W, GATE_COLS, 3 * D_MODEL)
_IN_OFFSETS = tuple(int(sum(_IN_WIDTHS[:i + 1])) for i in range(len(_IN_WIDTHS) - 1))
```

```python
import functools
import math

import jax
import jax.numpy as jnp
from jax import lax
from jax.experimental import pallas as pl
from jax.experimental.pallas import tpu as pltpu

F32 = jnp.float32
BF16 = jnp.bfloat16

D_MODEL = 1024
N_HEADS = 8
HEAD_W = 128
DA = 64
DKB = 64
CHUNK = 128
CONV_W = 3
N_EXPERTS = 16
D_FF = 2048
CAPACITY_FACTOR = 2
T5_BUCKETS = 32
T5_MAX_DIST = 128
ROPE_BASE = 10000.0
EPS = 1e-6
NEG = -1e30
GATE_COLS = 32
LOG2E = math.log2(math.e)
LANES = 128

VMEM_LIMIT = 56 * 1024 * 1024

NT_DIMS = (((1,), (1,)), ((), ()))
TN_DIMS = (((0,), (0,)), ((), ()))


def _params(sem):
    return pltpu.CompilerParams(dimension_semantics=sem, vmem_limit_bytes=VMEM_LIMIT)


def _log_sigmoid(x):
    return -(jnp.maximum(-x, 0.0) + jnp.log1p(jnp.exp(-jnp.abs(x))))


def _sigmoid(x):
    return 1.0 / (1.0 + jnp.exp(-x))


def _silu(x):
    return x * _sigmoid(x)


def _rms(x, g):
    return x * lax.rsqrt(jnp.mean(x * x, axis=-1, keepdims=True) + EPS) * g


def _layer_norm_head(h, gain):
    h = h - jnp.mean(h, axis=-1, keepdims=True)
    return h * lax.rsqrt(jnp.mean(h * h, axis=-1, keepdims=True) + EPS) * gain


def _norm_matmul_kernel(x_ref, g_ref, w_ref, o_ref, xn_ref):
    @pl.when(pl.program_id(1) == 0)
    def _():
        xn_ref[...] = _rms(x_ref[...], g_ref[...]).astype(BF16)

    o_ref[...] = jnp.dot(xn_ref[...], w_ref[...], preferred_element_type=F32).astype(o_ref.dtype)


def _norm_matmul(x, g, w, out_dtype, tm, tn):
    T, D = x.shape
    N = w.shape[1]
    return pl.pallas_call(
        _norm_matmul_kernel,
        grid=(T // tm, N // tn),
        in_specs=[
            pl.BlockSpec((tm, D), lambda i, j: (i, 0)),
            pl.BlockSpec((1, D), lambda i, j: (0, 0)),
            pl.BlockSpec((D, tn), lambda i, j: (0, j)),
        ],
        out_specs=pl.BlockSpec((tm, tn), lambda i, j: (i, j)),
        out_shape=jax.ShapeDtypeStruct((T, N), out_dtype),
        scratch_shapes=[pltpu.VMEM((tm, D), BF16)],
        compiler_params=_params(("parallel", "arbitrary")),
        name="norm_matmul",
    )(x, g, w)


def _norm_matmul2_kernel(x_ref, g_ref, w_ref, w2_ref, o_ref, o2_ref, xn_ref):
    @pl.when(pl.program_id(1) == 0)
    def _():
        xn = _rms(x_ref[...], g_ref[...]).astype(BF16)
        xn_ref[...] = xn
        o2_ref[...] = jnp.dot(xn, w2_ref[...], preferred_element_type=F32)

    o_ref[...] = jnp.dot(xn_ref[...], w_ref[...], preferred_element_type=F32).astype(o_ref.dtype)


def _norm_matmul2(x, g, w, w2, out_dtype, tm, tn):
    T, D = x.shape
    N, N2 = w.shape[1], w2.shape[1]
    return pl.pallas_call(
        _norm_matmul2_kernel,
        grid=(T // tm, N // tn),
        in_specs=[
            pl.BlockSpec((tm, D), lambda i, j: (i, 0)),
            pl.BlockSpec((1, D), lambda i, j: (0, 0)),
            pl.BlockSpec((D, tn), lambda i, j: (0, j)),
            pl.BlockSpec((D, N2), lambda i, j: (0, 0)),
        ],
        out_specs=[pl.BlockSpec((tm, tn), lambda i, j: (i, j)), pl.BlockSpec((tm, N2), lambda i, j: (i, 0))],
        out_shape=[jax.ShapeDtypeStruct((T, N), out_dtype), jax.ShapeDtypeStruct((T, N2), F32)],
        scratch_shapes=[pltpu.VMEM((tm, D), BF16)],
        compiler_params=_params(("parallel", "arbitrary")),
        name="norm_matmul2",
    )(x, g, w, w2)


def _add_norm_matmul_kernel(x_ref, r_ref, g_ref, w_ref, xs_ref, o_ref, xn_ref):
    @pl.when(pl.program_id(1) == 0)
    def _():
        xs = x_ref[...] + r_ref[...]
        xs_ref[...] = xs
        xn_ref[...] = _rms(xs, g_ref[...]).astype(BF16)

    o_ref[...] = jnp.dot(xn_ref[...], w_ref[...], preferred_element_type=F32).astype(o_ref.dtype)


def _add_norm_matmul(x, r, g, w, out_dtype, tm, tn):
    T, D = x.shape
    N = w.shape[1]
    row = pl.BlockSpec((tm, D), lambda i, j: (i, 0))
    return pl.pallas_call(
        _add_norm_matmul_kernel,
        grid=(T // tm, N // tn),
        in_specs=[row, row, pl.BlockSpec((1, D), lambda i, j: (0, 0)), pl.BlockSpec((D, tn), lambda i, j: (0, j))],
        out_specs=[row, pl.BlockSpec((tm, tn), lambda i, j: (i, j))],
        out_shape=[jax.ShapeDtypeStruct((T, D), F32), jax.ShapeDtypeStruct((T, N), out_dtype)],
        scratch_shapes=[pltpu.VMEM((tm, D), BF16)],
        compiler_params=_params(("parallel", "arbitrary")),
        name="add_norm_matmul",
    )(x, r, g, w)


def _t5_bias_kernel(tab_ref, bkt_ref, o_ref, *, tk, S):
    h = pl.program_id(0)
    ki = pl.program_id(1)
    bkt = bkt_ref[...]
    r = jnp.zeros((1, 2 * S), F32)
    for b in range(T5_BUCKETS):
        r = jnp.where(bkt == b, tab_ref[b, h] * LOG2E, r)
    x = jnp.broadcast_to(r, (tk, 2 * S))
    rolled = pltpu.roll(x, ki * tk + S, 1, stride=1, stride_axis=0)
    o_ref[0] = rolled[:, :S]


def _t5_bucket(rel):
    half = T5_BUCKETS // 2
    max_exact = half // 2
    offset = jnp.where(rel > 0, half, 0)
    n = jnp.abs(rel)
    large = max_exact + (jnp.log(jnp.maximum(n, 1).astype(F32) / max_exact)
                         / math.log(T5_MAX_DIST / max_exact) * (half - max_exact)).astype(jnp.int32)
    large = jnp.minimum(large, half - 1)
    return offset + jnp.where(n < max_exact, n, large)


def _t5_bias(table, S, tk):
    rel = S - jnp.arange(2 * S, dtype=jnp.int32)
    bkt = _t5_bucket(rel).astype(jnp.int32).reshape(1, 2 * S)
    return pl.pallas_call(
        functools.partial(_t5_bias_kernel, tk=tk, S=S),
        grid=(N_HEADS, S // tk),
        in_specs=[
            pl.BlockSpec(memory_space=pltpu.SMEM),
            pl.BlockSpec((1, 2 * S), lambda h, k: (0, 0)),
        ],
        out_specs=pl.BlockSpec((1, tk, S), lambda h, k: (h, k, 0)),
        out_shape=jax.ShapeDtypeStruct((N_HEADS, S, S), F32),
        compiler_params=_params(("parallel", "arbitrary")),
        name="t5_bias",
    )(table, bkt)


ONES_ROWS = 16


KEY_BLK = 256
BF16_ROWS = 16
PREP_GROUP = 16


def _attn_kernel(dl_ref, q_ref, k_ref, v_ref, b_ref, g_ref, o_ref, vt_ref, *, lam_init):
    @pl.when(pl.program_id(2) == 0)
    def _():
        vt_ref[:HEAD_W, :] = v_ref[0].astype(F32).T.astype(BF16)
        vt_ref[HEAD_W:, :] = jnp.ones((ONES_ROWS, vt_ref.shape[1]), BF16)

    dl = dl_ref[...]
    lam = (jnp.exp(jnp.sum(dl[0:1] * dl[1:2], axis=-1, keepdims=True))
           - jnp.exp(jnp.sum(dl[2:3] * dl[3:4], axis=-1, keepdims=True)) + lam_init)
    q = q_ref[0] * (DA ** -0.5)
    lane = lax.broadcasted_iota(jnp.int32, q.shape, 1)
    q_halves = (jnp.where(lane < DA, q, jnp.zeros_like(q)), jnp.where(lane >= DA, q, jnp.zeros_like(q)))
    key_blk = min(KEY_BLK, k_ref.shape[1])
    n_blk = k_ref.shape[1] // key_blk

    def score_block(half, j):
        rows = slice(j * key_blk, (j + 1) * key_blk)
        s = lax.dot_general(k_ref[0, rows, :], q_halves[half], NT_DIMS, preferred_element_type=F32)
        return (s + b_ref[0, rows, :]).astype(BF16)

    m = [None, None]
    acc = [None, None]
    scores = [None, None]
    for j in range(n_blk + 1):
        pending = scores
        scores = [score_block(half, j) for half in range(2)] if j < n_blk else [None, None]
        if j == 0:
            continue
        rows = slice((j - 1) * key_blk, j * key_blk)
        for half in range(2):
            s = pending[half]
            tile_max = jnp.max(s.reshape(key_blk // BF16_ROWS, BF16_ROWS, s.shape[1]), axis=0)
            m_blk = jnp.max(tile_max.astype(F32), axis=0, keepdims=True)
            m_new = m_blk if m[half] is None else jnp.maximum(m[half], m_blk)
            e = jnp.exp2(s - m_new.astype(BF16))
            pv = jnp.dot(vt_ref[:, rows], e, preferred_element_type=F32)
            acc[half] = pv if acc[half] is None else acc[half] * jnp.exp2(m[half] - m_new) + pv
            m[half] = m_new

    o = (acc[0][:HEAD_W] / acc[0][HEAD_W:HEAD_W + 1]
         - lam * (acc[1][:HEAD_W] / acc[1][HEAD_W:HEAD_W + 1]))
    y = o * lax.rsqrt(jnp.mean(o * o, axis=0, keepdims=True) + EPS) * g_ref[...] * (1.0 - lam_init)
    o_ref[0] = y.T.astype(o_ref.dtype)


def _diff_attention(za, bias_t, dl, gain_col, lam_init, B, S, tq):
    return pl.pallas_call(
        functools.partial(_attn_kernel, lam_init=lam_init),
        grid=(N_HEADS, B, S // tq),
        in_specs=[
            pl.BlockSpec((4, DA), lambda h, b, q: (0, 0)),
            pl.BlockSpec((1, tq, HEAD_W), lambda h, b, q: (b, q, h)),
            pl.BlockSpec((1, S, HEAD_W), lambda h, b, q: (b, 0, N_HEADS + h)),
            pl.BlockSpec((1, S, HEAD_W), lambda h, b, q: (b, 0, 2 * N_HEADS + h)),
            pl.BlockSpec((1, S, tq), lambda h, b, q: (h, 0, q)),
            pl.BlockSpec((HEAD_W, 1), lambda h, b, q: (h, 0)),
        ],
        out_specs=pl.BlockSpec((1, tq, HEAD_W), lambda h, b, q: (b, q, h)),
        out_shape=jax.ShapeDtypeStruct((B, S, D_MODEL), BF16),
        scratch_shapes=[pltpu.VMEM((HEAD_W + ONES_ROWS, S), BF16)],
        compiler_params=_params(("parallel", "parallel", "arbitrary")),
        name="diff_attention",
    )(dl, za, za, za, bias_t, gain_col)


def _retention_kernel(dec_ref, q_ref, k_ref, v_ref, bg_ref, cos_ref, sin_ref, g_ref, o_ref,
                      qs_ref, kt_ref, acc_ref, kvf_ref, kvb_ref, r_ref, *, S):
    L = CHUNK
    n = S // L
    lane = lax.broadcasted_iota(jnp.int32, (S, LANES), 1)
    first_half = (lane & (DKB // 2)) == 0
    cos = cos_ref[...]
    sin = sin_ref[...]

    def rotary(t):
        partner = jnp.where(first_half, pltpu.roll(t, LANES - DKB // 2, 1), pltpu.roll(t, DKB // 2, 1))
        return t * cos + partner * sin

    qs_ref[...] = rotary(q_ref[0])
    ks = rotary(k_ref[0]) * (DKB ** -0.5)
    for c in range(n):
        kt_ref[c] = ks[c * L:(c + 1) * L].T

    row = lax.broadcasted_iota(jnp.int32, (L, LANES), 0).astype(F32)
    col = lax.broadcasted_iota(jnp.int32, (L, LANES), 1).astype(F32)
    lane_l = lax.broadcasted_iota(jnp.int32, (L, LANES), 1)
    pos_row = col[0:1]
    diff = row - col
    heads = []
    for j in range(2):
        lg = _log_sigmoid(dec_ref[j])
        lgf, lgb = lg[0:1], lg[1:2]
        heads.append(dict(
            mask=(lane_l // DKB) == j,
            dmat=(jnp.where(diff >= 0, jnp.exp(lgf * jnp.maximum(diff, 0.0)), 0.0)
                  + jnp.where(diff <= 0, jnp.exp(lgb * jnp.maximum(-diff, 0.0)), 0.0)),
            xi_f=jnp.exp(lgf * (row + 1.0)), xi_b=jnp.exp(lgb * (L - row)),
            zeta_f=jnp.exp(lgf * (L - 1.0 - pos_row)), zeta_b=jnp.exp(lgb * pos_row),
            decay_f=jnp.exp(lgf * L), decay_b=jnp.exp(lgb * L)))

    def prepare(c, carry):
        sl = pl.ds(pl.multiple_of(c * L, L), L)
        qc = qs_ref[sl, :]
        kt = kt_ref[c]
        ktb = kt.astype(BF16)
        for j, hd in enumerate(heads):
            vj = v_ref[0, sl, j * HEAD_W:(j + 1) * HEAD_W]
            qj = jnp.where(hd["mask"], qc, 0.0).astype(BF16)
            inner = jnp.dot(qj, ktb, preferred_element_type=F32) * hd["dmat"]
            acc_ref[j, sl, :] = jnp.dot(inner.astype(BF16), vj, preferred_element_type=F32)
            kvf_ref[j, c] = jnp.dot((kt * hd["zeta_f"]).astype(BF16), vj, preferred_element_type=F32)
            kvb_ref[j, c] = jnp.dot((kt * hd["zeta_b"]).astype(BF16), vj, preferred_element_type=F32)
        return carry

    lax.fori_loop(0, n, prepare, 0, unroll=2)
    r_ref[...] = jnp.zeros_like(r_ref)

    def recur(i, carry):
        for j, hd in enumerate(heads):
            for rev in (False, True):
                c = (n - 1 - i) if rev else i
                sl = pl.ds(pl.multiple_of(c * L, L), L)
                xi, decay, kv_ref = ((hd["xi_b"], hd["decay_b"], kvb_ref) if rev
                                     else (hd["xi_f"], hd["decay_f"], kvf_ref))
                r = r_ref[2 * j + rev]
                qj = (jnp.where(hd["mask"], qs_ref[sl, :], 0.0) * xi).astype(BF16)
                acc_ref[j, sl, :] += jnp.dot(qj, r.astype(BF16), preferred_element_type=F32)
                r_ref[2 * j + rev] = decay * r + kv_ref[j, c]
        return carry

    lax.fori_loop(0, n, recur, 0, unroll=2)
    for j in range(2):
        cols = slice(j * HEAD_W, (j + 1) * HEAD_W)
        y = _silu(bg_ref[0, :, cols]) * _layer_norm_head(acc_ref[j], g_ref[:, cols])
        o_ref[0, :, cols] = y.astype(o_ref.dtype)


def _retention(za, zb, dec, cos, sin, gain, B, S):
    qk_blk = 3 * N_HEADS
    pair_w = 2 * HEAD_W
    n = S // CHUNK
    return pl.pallas_call(
        functools.partial(_retention_kernel, S=S),
        grid=(B, N_HEADS // 2),
        in_specs=[
            pl.BlockSpec((2, 2, LANES), lambda b, p: (p, 0, 0)),
            pl.BlockSpec((1, S, LANES), lambda b, p: (b, 0, qk_blk + p)),
            pl.BlockSpec((1, S, LANES), lambda b, p: (b, 0, qk_blk + N_HEADS // 2 + p)),
            pl.BlockSpec((1, S, pair_w), lambda b, p: (b, 0, 3 * N_HEADS // 2 + p)),
            pl.BlockSpec((1, S, pair_w), lambda b, p: (b, 0, 4 * N_HEADS // 2 + p)),
            pl.BlockSpec((S, LANES), lambda b, p: (0, 0)),
            pl.BlockSpec((S, LANES), lambda b, p: (0, 0)),
            pl.BlockSpec((1, pair_w), lambda b, p: (0, p)),
        ],
        out_specs=pl.BlockSpec((1, S, pair_w), lambda b, p: (b, 0, p)),
        out_shape=jax.ShapeDtypeStruct((B, S, D_MODEL), BF16),
        scratch_shapes=[pltpu.VMEM((S, LANES), F32), pltpu.VMEM((n, LANES, CHUNK), F32),
                        pltpu.VMEM((2, S, HEAD_W), F32),
                        pltpu.VMEM((2, n, LANES, HEAD_W), F32), pltpu.VMEM((2, n, LANES, HEAD_W), F32),
                        pltpu.VMEM((4, LANES, HEAD_W), F32)],
        compiler_params=_params(("parallel", "arbitrary")),
        name="retention",
    )(dec, zb, zb, za, zb, cos, sin, gain)


def _mlstm_kernel(q_ref, k_ref, v_ref, gt_ref, gb_ref, cwq_ref, cwk_ref, co_ref, gn_ref, o_ref,
                  qs_ref, kt_ref, gsc_ref, va_ref, sv_ref, kv_ref, bcb_ref, dmx_ref, sc_ref, hf_ref, hb_ref,
                  c_ref, n_ref, *, S):
    L = CHUNK
    n = S // L
    h = pl.program_id(1)

    rows = lax.broadcasted_iota(jnp.int32, (S, HEAD_W), 0)

    def conv_silu(u, w):
        prev = jnp.where(rows == 0, 0.0, pltpu.roll(u, 1, 0))
        nxt = jnp.where(rows == S - 1, 0.0, pltpu.roll(u, S - 1, 0))
        return _silu(w[0:1] * prev + w[1:2] * u + w[2:3] * nxt)

    qs_ref[...] = conv_silu(q_ref[0], cwq_ref[...]).astype(BF16)
    va_ref[:, :HEAD_W] = v_ref[0]
    va_ref[:, HEAD_W:] = jnp.ones_like(v_ref[0])

    ks = conv_silu(k_ref[0], cwk_ref[...]) * (HEAD_W ** -0.5)
    gates = gt_ref[0] + gb_ref[...]
    gsub = lax.broadcasted_iota(jnp.int32, (GATE_COLS, L), 0)
    for c in range(n):
        kt_ref[c] = ks[c * L:(c + 1) * L].T
        gt = gates[c * L:(c + 1) * L].T[:GATE_COLS]
        pick = lambda j: jnp.sum(jnp.where(gsub == j, gt, 0.0), axis=0, keepdims=True)
        gsc_ref[c] = jnp.concatenate(
            [pick(h), _log_sigmoid(pick(N_HEADS + h)), pick(2 * N_HEADS + h),
             _log_sigmoid(pick(3 * N_HEADS + h)), jnp.zeros((4, L), F32)], axis=0)

    ri = lax.broadcasted_iota(jnp.int32, (L, L), 0)
    ci = lax.broadcasted_iota(jnp.int32, (L, L), 1)
    eye = ri == ci
    ones_rhs = jnp.ones((2 * L, LANES), BF16)

    causal = ((ci <= ri), (ci >= ri))
    grp = math.gcd(n, PREP_GROUP)
    units = [(u, d) for u in range(grp) for d in range(2)]

    def prepare(i, carry):
        cs = [i * grp + u for u in range(grp)]
        sls = [pl.ds(pl.multiple_of(c * L, L), L) for c in cs]
        kts = [kt_ref[c] for c in cs]
        gs = [gsc_ref[c] for c in cs]
        vas = [va_ref[sl, :] for sl in sls]
        qks = [jnp.dot(qs_ref[sl, :], kt.astype(BF16), preferred_element_type=F32) for sl, kt in zip(sls, kts)]
        b_cbs = {}
        for u, d in units:
            lfm = jnp.where(causal[d], gs[u][2 * d + 1:2 * d + 2], 0.0)
            hi = lfm.astype(BF16)
            lo = (lfm - hi.astype(F32)).astype(BF16)
            b_cbs[u, d] = jnp.dot(jnp.concatenate([hi, lo], axis=1), ones_rhs, preferred_element_type=F32)
        s0s, ktws = {}, {}
        for u, d in units:
            b_cb = b_cbs[u, d]
            ii = gs[u][2 * d:2 * d + 1]
            b_row = jnp.sum(jnp.where(eye, b_cb, 0.0), axis=0, keepdims=True)
            b_last = b_cb[0:1] if d else b_cb[L - 1:L]
            r_row = ii - b_row
            dlog = jnp.where(causal[d], b_cb + r_row, NEG)
            dmax = jnp.max(dlog, axis=1, keepdims=True)
            s0s[u, d] = (qks[u] * jnp.exp(dlog - dmax)).astype(BF16)
            wlog = r_row + b_last[:, 0:1]
            wmax = jnp.max(wlog, axis=1, keepdims=True)
            ktws[u, d] = (kts[u] * jnp.exp(wlog - wmax)).astype(BF16)
            bcb_ref[d, cs[u]] = b_cb
            dmx_ref[d, cs[u]] = jnp.broadcast_to(dmax, (L, LANES))
            sc_ref[d, cs[u]] = jnp.concatenate(
                [b_last, jnp.broadcast_to(wmax, (1, LANES)), jnp.zeros((6, LANES), F32)], axis=0)
        for u, d in units:
            sv_ref[d, cs[u]] = jnp.dot(s0s[u, d], vas[u], preferred_element_type=F32)
        for u, d in units:
            kv_ref[d, cs[u]] = jnp.dot(ktws[u, d], vas[u], preferred_element_type=F32)
        return carry

    lax.fori_loop(0, n // grp, prepare, 0)

    c_ref[...] = jnp.zeros_like(c_ref)
    n_ref[...] = jnp.zeros_like(n_ref)

    def recur(i, ms):
        out = []
        for d in range(2):
            c = (n - 1 - i) if d else i
            m = ms[d]
            sl = pl.ds(pl.multiple_of(c * L, L), L)
            q = qs_ref[sl, :]
            b_cb, dmax, sv, kv, sc = bcb_ref[d, c], dmx_ref[d, c], sv_ref[d, c], kv_ref[d, c], sc_ref[d, c]
            b_last, wmax = sc[0:1], sc[1:2]
            inter = b_cb + m
            m_row = jnp.maximum(dmax, inter)
            a = jnp.exp(dmax - m_row)
            w_state = jnp.exp(inter - m_row)
            qc = jnp.dot(q, c_ref[d].astype(BF16), preferred_element_type=F32)
            qn = jnp.dot(q, n_ref[d].astype(BF16), preferred_element_type=F32)
            num = a * sv[:, :HEAD_W] + w_state * qc
            den = a * sv[:, HEAD_W:] + w_state * qn
            hh = num / jnp.maximum(jnp.abs(den), jnp.exp(-m_row))
            if d:
                hb_ref[sl, :] = hh
            else:
                hf_ref[sl, :] = hh
            m_new = jnp.maximum(b_last + m, wmax)
            beta = jnp.exp(wmax - m_new)
            decay = jnp.exp(b_last + m - m_new)
            c_ref[d] = decay * c_ref[d] + beta * kv[:, :HEAD_W]
            n_ref[d] = decay * n_ref[d] + beta * kv[:, HEAD_W:]
            out.append(m_new)
        return tuple(out)

    m0 = jnp.full((1, LANES), NEG, F32)
    lax.fori_loop(0, n, recur, (m0, m0), unroll=2)

    y = _sigmoid(co_ref[0]) * _layer_norm_head(hf_ref[...] + hb_ref[...], gn_ref[...])
    o_ref[0] = y.astype(o_ref.dtype)


def _mlstm(za, zb, zg, gate_b, cwq, cwk, gain, B, S):
    n = S // CHUNK
    per_chunk = lambda rows, cols: pltpu.VMEM((2, n, rows, cols), F32)
    return pl.pallas_call(
        functools.partial(_mlstm_kernel, S=S),
        grid=(B, N_HEADS),
        in_specs=[
            pl.BlockSpec((1, S, HEAD_W), lambda b, h: (b, 0, 5 * N_HEADS + h)),
            pl.BlockSpec((1, S, HEAD_W), lambda b, h: (b, 0, 6 * N_HEADS + h)),
            pl.BlockSpec((1, S, HEAD_W), lambda b, h: (b, 0, 4 * N_HEADS + h)),
            pl.BlockSpec((1, S, LANES), lambda b, h: (b, 0, 0)),
            pl.BlockSpec((1, LANES), lambda b, h: (0, 0)),
            pl.BlockSpec((CONV_W, HEAD_W), lambda b, h: (0, h)),
            pl.BlockSpec((CONV_W, HEAD_W), lambda b, h: (0, h)),
            pl.BlockSpec((1, S, HEAD_W), lambda b, h: (b, 0, 7 * N_HEADS + h)),
            pl.BlockSpec((1, HEAD_W), lambda b, h: (0, h)),
        ],
        out_specs=pl.BlockSpec((1, S, HEAD_W), lambda b, h: (b, 0, h)),
        out_shape=jax.ShapeDtypeStruct((B, S, D_MODEL), BF16),
        scratch_shapes=[
            pltpu.VMEM((S, HEAD_W), BF16),
            pltpu.VMEM((n, HEAD_W, CHUNK), F32),
            pltpu.VMEM((n, 8, CHUNK), F32),
            pltpu.VMEM((S, 2 * HEAD_W), BF16),
            per_chunk(CHUNK, 2 * HEAD_W),
            per_chunk(HEAD_W, 2 * HEAD_W),
            per_chunk(CHUNK, LANES),
            per_chunk(CHUNK, LANES),
            per_chunk(8, LANES),
            pltpu.VMEM((S, HEAD_W), F32), pltpu.VMEM((S, HEAD_W), F32),
            pltpu.VMEM((2, HEAD_W, HEAD_W), F32), pltpu.VMEM((2, HEAD_W, HEAD_W), F32),
        ],
        compiler_params=_params(("parallel", "arbitrary")),
        name="mlstm",
    )(zb, zb, za, zg, gate_b, cwq, cwk, zb, gain)


def _merge_kernel(x_ref, ya_ref, yb_ref, yc_ref, bgt_ref, gb_ref, wa_ref, wb_ref, wc_ref, wo_ref,
                  g2_ref, wr_ref, xo_ref, xn_ref, lg_ref):
    D = D_MODEL
    gate = lambda j: _sigmoid(bgt_ref[:, j * D:(j + 1) * D] + gb_ref[j:j + 1, :])
    mixed = (gate(0) * jnp.dot(ya_ref[...], wa_ref[...], preferred_element_type=F32)
             + gate(1) * jnp.dot(yb_ref[...], wb_ref[...], preferred_element_type=F32)
             + gate(2) * jnp.dot(yc_ref[...], wc_ref[...], preferred_element_type=F32))
    x = x_ref[...] + jnp.dot(mixed.astype(BF16), wo_ref[...], preferred_element_type=F32)
    xo_ref[...] = x
    xn = _rms(x, g2_ref[...]).astype(BF16)
    xn_ref[...] = xn
    lg_ref[...] = jnp.dot(xn, wr_ref[...], preferred_element_type=F32)


def _merge(x, ya, yb, yc, zb, gate_b, wa, wb, wc, wo, g2, wr, tm):
    T, D = x.shape
    row = lambda i: (i, 0)
    const = lambda i: (0, 0)
    return pl.pallas_call(
        _merge_kernel,
        grid=(T // tm,),
        in_specs=[
            pl.BlockSpec((tm, D), row), pl.BlockSpec((tm, D), row), pl.BlockSpec((tm, D), row),
            pl.BlockSpec((tm, D), row), pl.BlockSpec((tm, 3 * D), row), pl.BlockSpec((3, D), const),
            pl.BlockSpec((D, D), const), pl.BlockSpec((D, D), const), pl.BlockSpec((D, D), const),
            pl.BlockSpec((D, D), const), pl.BlockSpec((1, D), const), pl.BlockSpec((D, LANES), const),
        ],
        out_specs=[pl.BlockSpec((tm, D), row), pl.BlockSpec((tm, D), row), pl.BlockSpec((tm, LANES), row)],
        out_shape=[jax.ShapeDtypeStruct((T, D), F32), jax.ShapeDtypeStruct((T, D), BF16),
                   jax.ShapeDtypeStruct((T, LANES), F32)],
        compiler_params=_params(("parallel",)),
        name="merge",
    )(x, ya, yb, yc, zb, gate_b, wa, wb, wc, wo, g2, wr)


def _router_kernel(lg_ref, tri_ref, aff_ref, pos_ref, *, cap):
    E = N_EXPERTS
    logits = lg_ref[0].T[:E]
    e = jnp.exp(logits - jnp.max(logits, axis=0, keepdims=True))
    aff = e / jnp.sum(e, axis=0, keepdims=True)
    key = pltpu.bitcast(aff, jnp.int32)

    def refine(i, thr):
        cand = thr | (jnp.int32(1) << (30 - i))
        cnt = jnp.sum(jnp.where(key >= cand, 1.0, 0.0), axis=1, keepdims=True)
        return jnp.where(cnt >= cap, cand, thr)

    thr = lax.fori_loop(0, 31, refine, jnp.zeros((E, 1), jnp.int32))
    above = key > thr
    tied = key == thr
    need = cap - jnp.sum(jnp.where(above, 1.0, 0.0), axis=1, keepdims=True)
    tri = tri_ref[...]
    tie_rank = jnp.dot(jnp.where(tied, 1.0, 0.0).astype(BF16), tri, preferred_element_type=F32)
    chosen = jnp.where(above, 1.0, jnp.where(tied, jnp.where(tie_rank < need, 1.0, 0.0), 0.0))
    slot = jnp.dot(chosen.astype(BF16), tri, preferred_element_type=F32)
    aff_ref[0] = aff
    pos_ref[0] = jnp.where(chosen > 0.0, slot, -1.0)


def _router(logits, tri, B, S, cap):
    out = jax.ShapeDtypeStruct((B, N_EXPERTS, S), F32)
    blk = pl.BlockSpec((1, N_EXPERTS, S), lambda b: (b, 0, 0))
    return pl.pallas_call(
        functools.partial(_router_kernel, cap=cap),
        grid=(B,),
        in_specs=[pl.BlockSpec((1, S, LANES), lambda b: (b, 0, 0)), pl.BlockSpec((S, S), lambda b: (0, 0))],
        out_specs=[blk, blk],
        out_shape=[out, out],
        compiler_params=_params(("parallel",)),
        name="router",
    )(logits, tri)


def _moe_kernel(xn_ref, pos_ref, aff_ref, wg_ref, wu_ref, wd_ref, o_ref,
                sel_ref, xe_ref, y_ref, gate_ref, *, S, cap, row_blk):
    e, f = pl.program_id(1), pl.program_id(2)

    @pl.when((e == 0) & (f == 0))
    def _():
        o_ref[...] = jnp.zeros_like(o_ref)

    @pl.when(f == 0)
    def _():
        slot = lax.broadcasted_iota(jnp.int32, (cap, S), 0).astype(F32)
        hit = slot == pos_ref[0, 0]
        sel = jnp.where(hit, 1.0, 0.0).astype(BF16)
        sel_ref[...] = sel
        xe_ref[...] = jnp.dot(sel, xn_ref[0], preferred_element_type=F32).astype(BF16)
        gate = jnp.sum(jnp.where(hit, aff_ref[0, 0], 0.0), axis=1, keepdims=True)
        gate_ref[...] = jnp.broadcast_to(gate, gate_ref.shape)
        y_ref[...] = jnp.zeros_like(y_ref)

    xe = xe_ref[...]
    hid = (_silu(jnp.dot(xe, wg_ref[0], preferred_element_type=F32))
           * jnp.dot(xe, wu_ref[0], preferred_element_type=F32))
    y_ref[...] += jnp.dot(hid.astype(BF16), wd_ref[0], preferred_element_type=F32)

    @pl.when(f == pl.num_programs(2) - 1)
    def _():
        y = (y_ref[...] * gate_ref[:, 0:1]).astype(BF16)
        for r in range(S // row_blk):
            rows = slice(r * row_blk, (r + 1) * row_blk)
            o_ref[0, rows, :] += lax.dot_general(sel_ref[:, rows], y, TN_DIMS, preferred_element_type=F32)


def _moe(xn, pos, aff, wg, wu, wd, B, S, cap, tf):
    D = D_MODEL
    row_blk = min(S, 512)
    return pl.pallas_call(
        functools.partial(_moe_kernel, S=S, cap=cap, row_blk=row_blk),
        grid=(B, N_EXPERTS, D_FF // tf),
        in_specs=[
            pl.BlockSpec((1, S, D), lambda b, e, f: (b, 0, 0)),
            pl.BlockSpec((1, 1, 1, S), lambda b, e, f: (b, e, 0, 0)),
            pl.BlockSpec((1, 1, 1, S), lambda b, e, f: (b, e, 0, 0)),
            pl.BlockSpec((1, D, tf), lambda b, e, f: (e, 0, f)),
            pl.BlockSpec((1, D, tf), lambda b, e, f: (e, 0, f)),
            pl.BlockSpec((1, tf, D), lambda b, e, f: (e, f, 0)),
        ],
        out_specs=pl.BlockSpec((1, S, D), lambda b, e, f: (b, 0, 0)),
        out_shape=jax.ShapeDtypeStruct((B, S, D), F32),
        scratch_shapes=[pltpu.VMEM((cap, S), BF16), pltpu.VMEM((cap, D), BF16),
                        pltpu.VMEM((cap, D), F32), pltpu.VMEM((cap, LANES), F32)],
        compiler_params=_params(("parallel", "arbitrary", "arbitrary")),
        name="moe",
    )(xn, pos, aff, wg, wu, wd)


def _final_norm_kernel(x_ref, r_ref, g_ref, o_ref):
    o_ref[...] = _rms(x_ref[...] + r_ref[...], g_ref[...])


def _final_norm(x, r, g, tm):
    T, D = x.shape
    row = pl.BlockSpec((tm, D), lambda i: (i, 0))
    return pl.pallas_call(
        _final_norm_kernel,
        grid=(T // tm,),
        in_specs=[row, row, pl.BlockSpec((1, D), lambda i: (0, 0))],
        out_specs=row,
        out_shape=jax.ShapeDtypeStruct((T, D), F32),
        compiler_params=_params(("parallel",)),
        name="final_norm",
    )(x, r, g)


def _rotary_tables(S):
    half = DKB // 2
    freqs = ROPE_BASE ** (-jnp.arange(half, dtype=F32) / half)
    ang = jnp.arange(S, dtype=F32)[:, None] * freqs[None, :]
    cos, sin = jnp.cos(ang), jnp.sin(ang)
    reps = LANES // DKB
    return jnp.tile(cos, (1, 2 * reps)), jnp.tile(jnp.concatenate([-sin, sin], axis=1), (1, reps))


def _pad_cols(a, width):
    return jnp.pad(a, ((0, 0), (0, width - a.shape[1])))


def kernel(x, t5_bias, attn_norm_g, w_in, branch_gate_b, diff_lambda, diff_norm_g, ret_decay_logit, ret_norm_g, mlstm_conv_w, mlstm_gate_b, mlstm_norm_g, w_branch_a, w_branch_b, w_branch_c, w_out, ffn_norm_g, w_router, w_exp_gate, w_exp_up, w_exp_down, final_norm_g):
    B, S, D = x.shape
    assert D == D_MODEL and S % CHUNK == 0
    T = B * S
    depth = w_in.shape[0]
    cap = CAPACITY_FACTOR * S // N_EXPERTS
    tm = min(1024, T)
    tq = min(1024, S)
    tf = D_FF // 2

    bias = _t5_bias(t5_bias.astype(F32), S, min(256, S))
    cos, sin = _rotary_tables(S)
    tri = (jnp.arange(S)[:, None] < jnp.arange(S)[None, :]).astype(BF16)

    x = x.reshape(T, D).astype(F32)
    ffn = None
    for l in range(depth):
        lam_init = 0.8 - 0.6 * math.exp(-0.3 * l)
        (w_aq, w_ak, w_av, w_bq, w_bk, w_bv, w_bg, w_cq, w_ck, w_cv, w_co, w_cgt, w_bgt) = jnp.split(
            w_in[l], _IN_OFFSETS, axis=-1)
        wa_cols = jnp.concatenate([w_aq * LOG2E, w_ak, w_av, w_bv, w_cv], axis=1).astype(BF16)
        wb_cols = jnp.concatenate([w_bgt, w_bq, w_bk, w_bg, w_cq, w_ck, w_co], axis=1).astype(BF16)
        wg_cols = _pad_cols(w_cgt, LANES).astype(BF16)
        g1 = attn_norm_g[l].reshape(1, D)

        if ffn is None:
            za = _norm_matmul(x, g1, wa_cols, BF16, tm, wa_cols.shape[1] // 2)
        else:
            x, za = _add_norm_matmul(x, ffn, g1, wa_cols, BF16, tm, wa_cols.shape[1] // 2)
        za = za.reshape(B, S, -1)
        zb, zg = _norm_matmul2(x, g1, wb_cols, wg_cols, F32, tm, 2 * D)
        zg = zg.reshape(B, S, LANES)
        zb3 = zb.reshape(B, S, -1)

        ya = _diff_attention(za, bias, diff_lambda[l].astype(F32), diff_norm_g[l].reshape(D, 1), lam_init, B, S, tq)
        dec = jnp.broadcast_to(ret_decay_logit[l].astype(F32).T[:, :, None], (N_HEADS, 2, LANES))
        yb = _retention(za, zb3, dec, cos, sin, ret_norm_g[l].reshape(1, D), B, S)
        gate_b = _pad_cols(mlstm_gate_b[l].astype(F32).reshape(1, GATE_COLS), LANES)
        cw = mlstm_conv_w[l].astype(F32)
        yc = _mlstm(za, zb3, zg, gate_b, cw[:, :D], cw[:, D:], mlstm_norm_g[l].reshape(1, D), B, S)

        x, xn, logits = _merge(
            x, ya.reshape(T, D), yb.reshape(T, D), yc.reshape(T, D), zb, branch_gate_b[l].astype(F32),
            w_branch_a[l].astype(BF16), w_branch_b[l].astype(BF16), w_branch_c[l].astype(BF16),
            w_out[l].astype(BF16), ffn_norm_g[l].reshape(1, D), _pad_cols(w_router[l], LANES).astype(BF16),
            min(256, T))

        aff, pos = _router(logits.reshape(B, S, LANES), tri, B, S, cap)
        ffn = _moe(xn.reshape(B, S, D), pos.reshape(B, N_EXPERTS, 1, S),
                   aff.reshape(B, N_EXPERTS, 1, S), w_exp_gate[l].astype(BF16), w_exp_up[l].astype(BF16),
                   w_exp_down[l].astype(BF16), B, S, cap, tf).reshape(T, D)
    return _final_norm(x, ffn, final_norm_g.reshape(1, D), tm).reshape(B, S, D)


_IN_WIDTHS = (N_HEADS * 2 * DA, N_HEADS * 2 * DA, N_HEADS * HEAD_W, N_HEADS * DKB, N_HEADS * DKB,
              N_HEADS * HEAD_W, N_HEADS * HEAD_W, N_HEADS * HEAD_W, N_HEADS * HEAD_W, N_HEADS * HEAD_W,
              N_HEADS * HEAD_W, GATE_COLS, 3 * D_MODEL)
_IN_OFFSETS = tuple(int(sum(_IN_WIDTHS[:i + 1])) for i in range(len(_IN_WIDTHS) - 1))
```
